```python
import jax
import jax.numpy as jnp
from jax import lax
import numpy as np

D_MODEL = 1024
BATCH = 32
SEQ = 256
DEPTH = 4
DEC_BATCH = 8
DEC_SEQ = 4096
PAST_LEN = 256

MIX_W = D_MODEL
ATTN_W = MIX_W // 4
POOL_W = MIX_W // 4
CONV_W = MIX_W // 4
GM_W = MIX_W - ATTN_W - POOL_W - CONV_W
HEAD_DIM = 64
N_Q_HEADS = ATTN_W // HEAD_DIM
N_KV_HEADS = N_Q_HEADS // 2
Q_PER_KV = N_Q_HEADS // N_KV_HEADS
KV_W = N_KV_HEADS * HEAD_DIM
WINDOW = 128
ATTN_BLOCK = 128
GRID_W = 64
ROPE_BASE = 10000.0
ROPE_PAIRS = HEAD_DIM // 4
POOL_GROUPS = 4
POOL_GROUP_W = POOL_W // POOL_GROUPS
POOL_SIZES = (2, 4, 8, 16)
CONV_WIDTH = 31
GM_GROUPS = 4
GM_GROUP_W = GM_W // GM_GROUPS
GM_CHUNK = 128
D_FF = 4 * D_MODEL
N_MOD = 6
EPS = 1e-6
NEG_INF = -1e30
IN_SPLITS = (ATTN_W, ATTN_W + KV_W, ATTN_W + 2 * KV_W, ATTN_W + 2 * KV_W + POOL_W,
             ATTN_W + 2 * KV_W + POOL_W + 2 * CONV_W)
IN_W = IN_SPLITS[-1] + 2 * GM_W

kernel_name = 'hybrid_diffusion_parallel_heads_step'


def _rmsnorm(x, g):
    xf = x.astype(jnp.float32)
    y = xf * lax.rsqrt(jnp.mean(jnp.square(xf), axis=-1, keepdims=True) + EPS)
    return (y * g.astype(jnp.float32)).astype(x.dtype)


def _axial_rope_tables(rows):
    row = jnp.repeat(jnp.arange(rows), GRID_W).astype(jnp.float32)
    col = jnp.tile(jnp.arange(GRID_W), rows).astype(jnp.float32)
    inv = ROPE_BASE ** (-jnp.arange(ROPE_PAIRS, dtype=jnp.float32) / ROPE_PAIRS)
    ang = jnp.concatenate([row[:, None] * inv, col[:, None] * inv], axis=-1)
    return jnp.cos(ang), jnp.sin(ang)


def _apply_rope(x, cos, sin):
    xf = x.astype(jnp.float32)
    x1, x2 = xf[..., :HEAD_DIM // 2], xf[..., HEAD_DIM // 2:]
    cos = cos[None, :, None, :]
    sin = sin[None, :, None, :]
    return jnp.concatenate([x1 * cos - x2 * sin, x1 * sin + x2 * cos], axis=-1).astype(x.dtype)


def _softmax_with_sink(s, sink):
    sk = jnp.broadcast_to(sink.astype(jnp.float32).reshape(N_KV_HEADS, Q_PER_KV, 1, 1), s.shape[:-1] + (1,))
    p = jax.nn.softmax(jnp.concatenate([s, sk], axis=-1), axis=-1)
    return p[..., :-1]


def _context_attention(q, k, v, sink):
    b, n, _, _ = q.shape
    nb = n // ATTN_BLOCK
    scale = HEAD_DIM ** -0.5
    qb = jnp.moveaxis(q.reshape(b, nb, ATTN_BLOCK, N_KV_HEADS, Q_PER_KV, HEAD_DIM), 1, 0)

    def one_block(qblk):
        s = jnp.einsum('bqkgd,bskd->bkgqs', qblk, k).astype(jnp.float32) * scale
        p = _softmax_with_sink(s, sink)
        return jnp.einsum('bkgqs,bskd->bqkgd', p.astype(v.dtype), v)

    o = lax.map(one_block, qb)
    return jnp.moveaxis(o, 0, 1).reshape(b, n, ATTN_W)


def _latent_attention(q, k, v, ck, cv, sink):
    b, n, _, _ = q.shape
    nb = n // ATTN_BLOCK
    scale = HEAD_DIM ** -0.5
    pad = ((0, 0), (ATTN_BLOCK, ATTN_BLOCK), (0, 0), (0, 0))
    kp = jnp.pad(k, pad).reshape(b, nb + 2, ATTN_BLOCK, N_KV_HEADS, HEAD_DIM)
    vp = jnp.pad(v, pad).reshape(b, nb + 2, ATTN_BLOCK, N_KV_HEADS, HEAD_DIM)
    kb = jnp.concatenate([kp[:, :-2], kp[:, 1:-1], kp[:, 2:]], axis=2)
    vb = jnp.concatenate([vp[:, :-2], vp[:, 1:-1], vp[:, 2:]], axis=2)
    qb = q.reshape(b, nb, ATTN_BLOCK, N_KV_HEADS, Q_PER_KV, HEAD_DIM)
    blk = jnp.arange(nb)[:, None, None]
    qpos = blk * ATTN_BLOCK + jnp.arange(ATTN_BLOCK)[None, :, None]
    kpos = (blk - 1) * ATTN_BLOCK + jnp.arange(3 * ATTN_BLOCK)[None, None, :]
    valid = (jnp.abs(qpos - kpos) <= WINDOW) & (kpos >= 0) & (kpos < n)
    s_loc = jnp.einsum('bnqkgd,bnskd->bnkgqs', qb, kb).astype(jnp.float32) * scale
    s_loc = jnp.where(valid[None, :, None, None], s_loc, NEG_INF)
    s_ctx = jnp.einsum('bnqkgd,bskd->bnkgqs', qb, ck).astype(jnp.float32) * scale
    p = _softmax_with_sink(jnp.concatenate([s_loc, s_ctx], axis=-1), sink).astype(v.dtype)
    n_loc = 3 * ATTN_BLOCK
    o = (jnp.einsum('bnkgqs,bnskd->bnqkgd', p[..., :n_loc], vb)
         + jnp.einsum('bnkgqs,bskd->bnqkgd', p[..., n_loc:], cv))
    return o.reshape(b, n, ATTN_W)


def _centred_mean(x, size):
    n = x.shape[1]
    xf = x.astype(jnp.float32)
    cs = jnp.pad(jnp.cumsum(xf, axis=1), ((0, 0), (1, 0), (0, 0)))
    t = jnp.arange(n)
    half = size // 2
    lo = jnp.clip(t - half, 0, n)
    hi = jnp.clip(t + half, 0, n)
    cnt = (hi - lo).astype(jnp.float32)
    return (cs[:, hi] - cs[:, lo]) / cnt[None, :, None]


def _pool_mixer(xp, pool_w, pool_scale):
    b, n, _ = xp.shape
    groups = jnp.split(xp, POOL_GROUPS, axis=-1)
    pooled = jnp.concatenate([_centred_mean(g, s) for g, s in zip(groups, POOL_SIZES)], axis=-1)
    pooled = pooled.astype(xp.dtype) - xp
    y = jnp.einsum('blgc,gcd->blgd', pooled.reshape(b, n, POOL_GROUPS, POOL_GROUP_W), pool_w)
    return y.reshape(b, n, POOL_W) * pool_scale


def _conv_mixer(xc, conv_dw, conv_b, conv_norm, conv_pw):
    a, gate = jnp.split(xc, 2, axis=-1)
    u = a * jax.nn.sigmoid(gate)
    half = CONV_WIDTH // 2
    y = lax.conv_general_dilated(u, conv_dw[:, None, :], window_strides=(1,),
                                 padding=[(half, half)], dimension_numbers=('NWC', 'WIO', 'NWC'),
                                 feature_group_count=CONV_W) + conv_b
    y = jax.nn.silu(_rmsnorm(y, conv_norm))
    return y @ conv_pw


def _gmlp_mixer(xg, gm_norm, gm_ws, gm_b):
    b, n, _ = xg.shape
    u, v = jnp.split(jax.nn.gelu(xg), 2, axis=-1)
    v = _rmsnorm(v, gm_norm)
    vc = v.reshape(b, n // GM_CHUNK, GM_CHUNK, GM_GROUPS, GM_GROUP_W)
    sv = jnp.einsum('gpq,bnqgc->bnpgc', gm_ws, vc) + gm_b.T[None, None, :, :, None]
    return u * sv.reshape(b, n, GM_W)


def _token_mixers(h, p, ctx_kv, rope):
    b, n, _ = h.shape
    proj = h @ p['w_in']
    q, k, v, xp, xc, xg = jnp.split(proj, IN_SPLITS, axis=-1)
    q = q.reshape(b, n, N_Q_HEADS, HEAD_DIM)
    k = k.reshape(b, n, N_KV_HEADS, HEAD_DIM)
    v = v.reshape(b, n, N_KV_HEADS, HEAD_DIM)
    if ctx_kv is None:
        attn = _context_attention(q, k, v, p['attn_sink'])
    else:
        cos, sin = rope
        q = _apply_rope(q, cos, sin)
        k = _apply_rope(k, cos, sin)
        attn = _latent_attention(q, k, v, ctx_kv[0], ctx_kv[1], p['attn_sink'])
    pool = _pool_mixer(xp, p['pool_w'], p['pool_scale'])
    conv = _conv_mixer(xc, p['conv_dw'], p['conv_b'], p['conv_norm'], p['conv_pw'])
    gm = _gmlp_mixer(xg, p['gm_norm'], p['gm_ws'], p['gm_b'])
    mix = jnp.concatenate([attn, pool, conv, gm], axis=-1) @ p['w_out']
    return mix, k, v


def _layer(x, mod, p, ctx_kv, rope):
    sh1, sc1, g1, sh2, sc2, g2 = jnp.split(mod[:, None, :].astype(x.dtype), N_MOD, axis=-1)
    h = _rmsnorm(x, p['norm1']) * (1 + sc1) + sh1
    mix, k, v = _token_mixers(h, p, ctx_kv, rope)
    x = x + g1 * mix
    h = _rmsnorm(x, p['norm2']) * (1 + sc2) + sh2
    f = jnp.square(jax.nn.relu(h @ p['w_mlp1'])) @ p['w_mlp2']
    x = x + g2 * f
    return x, k, v


def setup_inputs(seed: int = 0) -> dict:
    key = jax.random.key(seed)
    ks = jax.random.split(key, 25)

    def nrm(k, shape, scale):
        return jax.random.normal(k, shape, jnp.float32) * scale

    def gain(k, shape):
        return 1.0 + 0.05 * jax.random.normal(k, shape, jnp.float32)

    cache_shape = (DEC_BATCH, DEPTH, PAST_LEN, N_KV_HEADS, HEAD_DIM)
    return {
        'x_prompt': nrm(ks[0], (BATCH, SEQ, D_MODEL), 1.0),
        'x_sample': nrm(ks[1], (DEC_BATCH, DEC_SEQ, D_MODEL), 1.0),
        'cache_k': nrm(ks[2], cache_shape, 1.0),
        'cache_v': nrm(ks[3], cache_shape, 1.0),
        'c': nrm(ks[4], (DEC_BATCH, D_MODEL), 1.0),
        'c_ctx': nrm(ks[5], (D_MODEL,), 1.0),
        'w_ada': nrm(ks[6], (DEPTH, D_MODEL, N_MOD * D_MODEL), 0.5 * D_MODEL ** -0.5),
        'b_ada': nrm(ks[7], (DEPTH, N_MOD * D_MODEL), 0.02),
        'norm1': gain(ks[8], (DEPTH, D_MODEL)),
        'norm2': gain(ks[9], (DEPTH, D_MODEL)),
        'w_in': nrm(ks[10], (DEPTH, D_MODEL, IN_W), D_MODEL ** -0.5),
        'w_out': nrm(ks[11], (DEPTH, MIX_W, D_MODEL), MIX_W ** -0.5),
        'attn_sink': nrm(ks[12], (DEPTH, N_Q_HEADS), 0.5),
        'pool_w': nrm(ks[13], (DEPTH, POOL_GROUPS, POOL_GROUP_W, POOL_GROUP_W), POOL_GROUP_W ** -0.5),
        'pool_scale': gain(ks[14], (DEPTH, POOL_W)),
        'conv_dw': nrm(ks[15], (DEPTH, CONV_WIDTH, CONV_W), CONV_WIDTH ** -0.5),
        'conv_b': nrm(ks[16], (DEPTH, CONV_W), 0.02),
        'conv_norm': gain(ks[17], (DEPTH, CONV_W)),
        'conv_pw': nrm(ks[18], (DEPTH, CONV_W, CONV_W), CONV_W ** -0.5),
        'gm_norm': gain(ks[19], (DEPTH, GM_W // 2 * 2 // 2 * 1 if False else GM_W)),
        'gm_ws': nrm(ks[20], (DEPTH, GM_GROUPS, GM_CHUNK, GM_CHUNK), GM_CHUNK ** -0.5),
        'gm_b': gain(ks[21], (DEPTH, GM_GROUPS, GM_CHUNK)),
        'w_mlp1': nrm(ks[22], (DEPTH, D_MODEL, D_FF), D_MODEL ** -0.5),
        'w_mlp2': nrm(ks[23], (DEPTH, D_FF, D_MODEL), D_FF ** -0.5),
        'final_norm': gain(ks[24], (D_MODEL,)),
    }


def reference(x_prompt, x_sample, cache_k, cache_v, c, c_ctx, w_ada, b_ada, norm1, norm2,
              w_in, w_out, attn_sink, pool_w, pool_scale, conv_dw, conv_b, conv_norm, conv_pw,
              gm_norm, gm_ws, gm_b, w_mlp1, w_mlp2, final_norm):
    rows = x_sample.shape[1] // GRID_W
    rope = _axial_rope_tables(rows)
    silu_ctx = jax.nn.silu(c_ctx)[None, :]
    silu_c = jax.nn.silu(c)
    xc_stream = x_prompt
    xs_stream = x_sample
    ks_out = []
    vs_out = []
    for l in range(DEPTH):
        p = {'norm1': norm1[l], 'norm2': norm2[l], 'w_in': w_in[l], 'w_out': w_out[l],
             'attn_sink': attn_sink[l], 'pool_w': pool_w[l], 'pool_scale': pool_scale[l],
             'conv_dw': conv_dw[l], 'conv_b': conv_b[l], 'conv_norm': conv_norm[l],
             'conv_pw': conv_pw[l], 'gm_norm': gm_norm[l], 'gm_ws': gm_ws[l], 'gm_b': gm_b[l],
             'w_mlp1': w_mlp1[l], 'w_mlp2': w_mlp2[l]}
        mod_ctx = silu_ctx @ w_ada[l] + b_ada[l]
        mod_lat = silu_c @ w_ada[l] + b_ada[l]
        xc_stream, k_ctx, v_ctx = _layer(xc_stream, mod_ctx, p, None, None)
        ks_out.append(k_ctx)
        vs_out.append(v_ctx)
        xs_stream, _, _ = _layer(xs_stream, mod_lat, p, (cache_k[:, l], cache_v[:, l]), rope)
    y_prompt = _rmsnorm(xc_stream, final_norm)
    y_sample = _rmsnorm(xs_stream, final_norm)
    new_k = jnp.stack(ks_out, axis=1)
    new_v = jnp.stack(vs_out, axis=1)
    return (y_prompt, y_sample, new_k, new_v)
```

```python
import functools

import jax
import jax.numpy as jnp
from jax import lax
from jax.experimental import pallas as pl
from jax.experimental.pallas import tpu as pltpu

D_MODEL = 1024
DEPTH = 4
ATTN_W = 256
POOL_W = 256
CONV_W = 256
GM_W = 256
HEAD_DIM = 64
N_Q_HEADS = 4
N_KV_HEADS = 2
Q_PER_KV = 2
KV_W = 128
ATTN_BLOCK = 128
GRID_W = 64
ROPE_BASE = 10000.0
ROPE_PAIRS = 16
POOL_GROUPS = 4
POOL_SIZES = (2, 4, 8, 16)
CONV_WIDTH = 31
CONV_HALF = 15
GM_GROUPS = 4
GM_GROUP_W = 64
GM_CHUNK = 128
D_FF = 4096
N_MOD = 6
EPS = 1e-6
NEG_INF = -1e30
IN_W = 1792
C_Q, C_K, C_V, C_POOL, C_CONV, C_GM = 0, 256, 384, 512, 768, 1280

HALO = 128
MARGIN = 16
MOD_ROWS = 16
FF_CHUNK = 1024
VMEM_LIMIT_BYTES = 58 * 1024 * 1024


def _rms(x, g):
    ms = jnp.mean(x * x, axis=-1, keepdims=True)
    return x * lax.rsqrt(ms + EPS) * g


def _dot(a, b):
    return jnp.dot(a, b, preferred_element_type=jnp.float32)


def _dot_nt(a, b):
    return lax.dot_general(a, b, (((1,), (1,)), ((), ())), preferred_element_type=jnp.float32)


def _ada_kernel(cond_ref, w_ref, b_ref, o_ref):
    s = cond_ref[...]
    s = s * jax.nn.sigmoid(s)
    o_ref[0] = _dot(s.astype(jnp.bfloat16), w_ref[0].astype(jnp.bfloat16)) + b_ref[0]


def _ada_mod(cond, w_ada, b_ada):
    tn = 1536
    nt = (N_MOD * D_MODEL) // tn
    return pl.pallas_call(
        _ada_kernel,
        grid=(DEPTH, nt),
        in_specs=[
            pl.BlockSpec((MOD_ROWS, D_MODEL), lambda l, j: (0, 0)),
            pl.BlockSpec((1, D_MODEL, tn), lambda l, j: (l, 0, j)),
            pl.BlockSpec((1, 1, tn), lambda l, j: (l, 0, j)),
        ],
        out_specs=pl.BlockSpec((1, MOD_ROWS, tn), lambda l, j: (l, 0, j)),
        out_shape=jax.ShapeDtypeStruct((DEPTH, MOD_ROWS, N_MOD * D_MODEL), jnp.float32),
        compiler_params=pltpu.CompilerParams(
            dimension_semantics=("arbitrary", "arbitrary"),
            vmem_limit_bytes=VMEM_LIMIT_BYTES),
        name="ada_mod",
    )(cond, w_ada, b_ada.reshape(DEPTH, 1, N_MOD * D_MODEL))


def _layer_kernel(*refs, layer, tile, halo, seq_len, latent, last):
    it = iter(refs)
    if halo:
        xprev_ref = next(it)
    x_ref = next(it)
    if halo:
        xnext_ref = next(it)
    mod_ref = next(it)
    norm1_ref = next(it)
    norm2_ref = next(it)
    w_in_ref = next(it)
    w_out_ref = next(it)
    sink_ref = next(it)
    pool_w_ref = next(it)
    pool_scale_ref = next(it)
    conv_dw_ref = next(it)
    conv_b_ref = next(it)
    conv_norm_ref = next(it)
    conv_pw_ref = next(it)
    gm_norm_ref = next(it)
    gm_ws_ref = next(it)
    gm_b_ref = next(it)
    w1_ref = next(it)
    w2_ref = next(it)
    if latent:
        cos_ref = next(it)
        sin_ref = next(it)
        ck_ref = next(it)
        cv_ref = next(it)
    if last:
        fnorm_ref = next(it)
    out_ref = next(it)
    if not latent:
        kout_ref = next(it)
        vout_ref = next(it)
    h_scr = next(it)
    proj_scr = next(it)
    q_scr = next(it)
    k_scr = next(it)
    v_scr = next(it)
    slab_scr = next(it)
    mix_scr = next(it)
    acc_scr = next(it)

    rows = tile + 2 * halo
    ti = pl.program_id(1)
    n_tiles = seq_len // tile
    is_first = ti == 0
    is_last_tile = ti == n_tiles - 1

    mod = mod_ref[0]
    sh1 = mod[:, 0 * D_MODEL:1 * D_MODEL]
    sc1 = mod[:, 1 * D_MODEL:2 * D_MODEL]
    g1 = mod[:, 2 * D_MODEL:3 * D_MODEL]
    sh2 = mod[:, 3 * D_MODEL:4 * D_MODEL]
    sc2 = mod[:, 4 * D_MODEL:5 * D_MODEL]
    g2 = mod[:, 5 * D_MODEL:6 * D_MODEL]
    norm1 = norm1_ref[0]
    norm2 = norm2_ref[0]

    def norm_mod1(x):
        return (_rms(x, norm1) * (1.0 + sc1) + sh1).astype(jnp.bfloat16)

    if halo:
        h_scr[0:halo, :] = norm_mod1(xprev_ref[0])
        h_scr[halo + tile:rows, :] = norm_mod1(xnext_ref[0])
    h_scr[halo:halo + tile, :] = norm_mod1(x_ref[0])
    proj_scr[...] = _dot(h_scr[...], w_in_ref[0])

    c0 = halo
    c1 = halo + tile

    scale = HEAD_DIM ** -0.5
    if latent:
        row0 = ti * tile
        cos_t = cos_ref[pl.ds(pl.multiple_of(row0, tile), rows), :]
        sin_t = sin_ref[pl.ds(pl.multiple_of(row0, tile), rows), :]
        lane = lax.broadcasted_iota(jnp.int32, (1, 128), 1)
        first_half = (lane % HEAD_DIM) < (HEAD_DIM // 2)

        def rope(z, cs, sn):
            swapped = jnp.where(first_half, pltpu.roll(z, 96, 1), pltpu.roll(z, 32, 1))
            return z * cs + swapped * sn

    k_all = proj_scr[:, C_K:C_K + KV_W]
    if not latent:
        kout_ref[0] = k_all
        vout_ref[0] = proj_scr[:, C_V:C_V + KV_W]
    if latent:
        k_all = rope(k_all, cos_t, sin_t)
    k_bf = k_all.astype(jnp.bfloat16)
    v_bf = proj_scr[:, C_V:C_V + KV_W].astype(jnp.bfloat16)
    for g in range(N_KV_HEADS):
        k_scr[g] = k_bf[:, g * HEAD_DIM:(g + 1) * HEAD_DIM]
        v_scr[g] = v_bf[:, g * HEAD_DIM:(g + 1) * HEAD_DIM]
    for half in range(2):
        qh = proj_scr[c0:c1, C_Q + half * 128:C_Q + (half + 1) * 128]
        if latent:
            qh = rope(qh, cos_t[c0:c1], sin_t[c0:c1])
        qh = (qh * scale).astype(jnp.bfloat16)
        for s in range(2):
            q_scr[2 * half + s] = qh[:, s * HEAD_DIM:(s + 1) * HEAD_DIM]

    if latent:
        ck_bf = ck_ref[0, 0].astype(jnp.bfloat16)
        cv_bf = cv_ref[0, 0].astype(jnp.bfloat16)
        r_i = lax.broadcasted_iota(jnp.int32, (2 * ATTN_BLOCK, 3 * ATTN_BLOCK), 0) % ATTN_BLOCK
        c_i = lax.broadcasted_iota(jnp.int32, (2 * ATTN_BLOCK, 3 * ATTN_BLOCK), 1)
        band_ok = jnp.abs(r_i - (c_i - ATTN_BLOCK)) <= HALO
        n_qblk = tile // ATTN_BLOCK
        head_row = lax.broadcasted_iota(jnp.int32, (2 * ATTN_BLOCK, 1), 0) < ATTN_BLOCK
        for j in range(n_qblk):
            bias = jnp.where(band_ok, 0.0, NEG_INF)
            if j == 0:
                lo = jnp.where(is_first, ATTN_BLOCK, 0)
                bias = jnp.where(c_i >= lo, bias, NEG_INF)
            if j == n_qblk - 1:
                hi = jnp.where(is_last_tile, 2 * ATTN_BLOCK, 3 * ATTN_BLOCK)
                bias = jnp.where(c_i < hi, bias, NEG_INF)
            js = slice(j * ATTN_BLOCK, (j + 1) * ATTN_BLOCK)
            ks = slice(j * ATTN_BLOCK, (j + 3) * ATTN_BLOCK)
            for g in range(N_KV_HEADS):
                q2 = jnp.concatenate([q_scr[2 * g, js, :], q_scr[2 * g + 1, js, :]], axis=0)
                s_loc = _dot_nt(q2, k_scr[g, ks, :]) + bias
                s_ctx = _dot_nt(q2, ck_bf[:, g * HEAD_DIM:(g + 1) * HEAD_DIM])
                sink = jnp.where(head_row, sink_ref[layer, 2 * g], sink_ref[layer, 2 * g + 1])
                m = jnp.maximum(jnp.maximum(jnp.max(s_loc, axis=-1, keepdims=True),
                                            jnp.max(s_ctx, axis=-1, keepdims=True)), sink)
                p_loc = jnp.exp(s_loc - m)
                p_ctx = jnp.exp(s_ctx - m)
                den = (jnp.sum(p_loc, axis=-1, keepdims=True) + jnp.sum(p_ctx, axis=-1, keepdims=True)
                       + jnp.exp(sink - m))
                o = (_dot(p_loc.astype(jnp.bfloat16), v_scr[g, ks, :])
                     + _dot(p_ctx.astype(jnp.bfloat16), cv_bf[:, g * HEAD_DIM:(g + 1) * HEAD_DIM]))
                o = (o / den).astype(jnp.bfloat16)
                mix_scr[js, (2 * g) * HEAD_DIM:(2 * g + 1) * HEAD_DIM] = o[:ATTN_BLOCK]
                mix_scr[js, (2 * g + 1) * HEAD_DIM:(2 * g + 2) * HEAD_DIM] = o[ATTN_BLOCK:]
    else:
        head_row = lax.broadcasted_iota(jnp.int32, (2 * tile, 1), 0) < tile
        for g in range(N_KV_HEADS):
            q2 = jnp.concatenate([q_scr[2 * g], q_scr[2 * g + 1]], axis=0)
            s = _dot_nt(q2, k_scr[g])
            sink = jnp.where(head_row, sink_ref[layer, 2 * g], sink_ref[layer, 2 * g + 1])
            m = jnp.maximum(jnp.max(s, axis=-1, keepdims=True), sink)
            p = jnp.exp(s - m)
            den = jnp.sum(p, axis=-1, keepdims=True) + jnp.exp(sink - m)
            o = (_dot(p.astype(jnp.bfloat16), v_scr[g]) / den).astype(jnp.bfloat16)
            mix_scr[:, (2 * g) * HEAD_DIM:(2 * g + 1) * HEAD_DIM] = o[:tile]
            mix_scr[:, (2 * g + 1) * HEAD_DIM:(2 * g + 2) * HEAD_DIM] = o[tile:]

    srows = tile + 2 * MARGIN

    def fill_slab(fn):
        slab_scr[MARGIN:MARGIN + tile, :] = fn(c0, c1)
        zeros = jnp.zeros((MARGIN, 256), jnp.float32)
        if halo:
            left = fn(c0 - MARGIN, c0)
            right = fn(c1, c1 + MARGIN)
            slab_scr[0:MARGIN, :] = jnp.where(is_first, zeros, left)
            slab_scr[MARGIN + tile:srows, :] = jnp.where(is_last_tile, zeros, right)
        else:
            slab_scr[0:MARGIN, :] = zeros
            slab_scr[MARGIN + tile:srows, :] = zeros

    fill_slab(lambda a, b: proj_scr[a:b, C_POOL:C_POOL + POOL_W])
    xs = slab_scr[...]
    sums = []
    acc = xs
    step = 1
    for size in POOL_SIZES:
        acc = acc + pltpu.roll(acc, step, 0)
        step = size
        shift = size // 2 - 1
        win = acc if shift == 0 else pltpu.roll(acc, srows - shift, 0)
        sums.append(win[MARGIN:MARGIN + tile])
    lane256 = lax.broadcasted_iota(jnp.int32, (1, POOL_W), 1)
    grp = lane256 // (POOL_W // POOL_GROUPS)
    wsum = jnp.where(grp == 0, sums[0], jnp.where(grp == 1, sums[1], jnp.where(grp == 2, sums[2], sums[3])))
    half_w = jnp.where(grp == 0, 1, jnp.where(grp == 1, 2, jnp.where(grp == 2, 4, 8)))
    tpos = ti * tile + lax.broadcasted_iota(jnp.int32, (tile, 1), 0)
    cnt = (jnp.minimum(tpos + half_w, seq_len) - jnp.maximum(tpos - half_w, 0)).astype(jnp.float32)
    pooled = wsum / cnt - xs[MARGIN:MARGIN + tile]
    pool_out = _dot(pooled.astype(jnp.bfloat16), pool_w_ref[0]) * pool_scale_ref[0]
    mix_scr[:, ATTN_W:ATTN_W + POOL_W] = pool_out.astype(jnp.bfloat16)

    def glu(a, b):
        av = proj_scr[a:b, C_CONV:C_CONV + CONV_W]
        gv = proj_scr[a:b, C_CONV + CONV_W:C_CONV + 2 * CONV_W]
        return av * jax.nn.sigmoid(gv)

    fill_slab(glu)
    y = jnp.zeros((tile, CONV_W), jnp.float32) + conv_b_ref[0]
    for j in range(CONV_WIDTH):
        off = MARGIN - CONV_HALF + j
        y = y + slab_scr[off:off + tile, :] * conv_dw_ref[0, j:j + 1, :]
    y = _rms(y, conv_norm_ref[0])
    y = y * jax.nn.sigmoid(y)
    conv_out = _dot(y.astype(jnp.bfloat16), conv_pw_ref[0])
    mix_scr[:, ATTN_W + POOL_W:ATTN_W + POOL_W + CONV_W] = conv_out.astype(jnp.bfloat16)

    gu = jax.nn.gelu(proj_scr[c0:c1, C_GM:C_GM + GM_W])
    gv = jax.nn.gelu(proj_scr[c0:c1, C_GM + GM_W:C_GM + 2 * GM_W])
    gv = _rms(gv, gm_norm_ref[0]).astype(jnp.bfloat16)
    gm_b = gm_b_ref[0]
    for c in range(tile // GM_CHUNK):
        cs = slice(c * GM_CHUNK, (c + 1) * GM_CHUNK)
        parts = []
        for g in range(GM_GROUPS):
            parts.append(_dot(gm_ws_ref[0, g], gv[cs, g * GM_GROUP_W:(g + 1) * GM_GROUP_W]))
        sv = jnp.concatenate(parts, axis=-1) + gm_b
        mix_scr[cs, ATTN_W + POOL_W + CONV_W:D_MODEL] = (gu[cs] * sv).astype(jnp.bfloat16)

    x1 = x_ref[0] + g1 * _dot(mix_scr[...], w_out_ref[0])
    out_ref[0] = x1
    h_scr[0:tile, :] = (_rms(x1, norm2) * (1.0 + sc2) + sh2).astype(jnp.bfloat16)

    for c in range(D_FF // FF_CHUNK):
        a = _dot(h_scr[0:tile, :], w1_ref[0, :, c * FF_CHUNK:(c + 1) * FF_CHUNK])
        a = jnp.square(jnp.maximum(a, 0.0)).astype(jnp.bfloat16)
        f = _dot(a, w2_ref[0, c * FF_CHUNK:(c + 1) * FF_CHUNK, :])
        if c == 0:
            acc_scr[...] = f
        else:
            acc_scr[...] += f
    x2 = out_ref[0] + g2 * acc_scr[...]
    if last:
        x2 = _rms(x2, fnorm_ref[...])
    out_ref[0] = x2


def _const_spec(shape, index):
    return pl.BlockSpec(shape, lambda b, t: index, pipeline_mode=pl.Buffered(1))


def _layer_call(x, layer, mod_all, params, *, tile, latent, last, rope=None, cache=None, final_norm=None):
    batch, seq_len, _ = x.shape
    halo = HALO if latent else 0
    rows = tile + 2 * halo
    n_tiles = seq_len // tile
    assert seq_len % tile == 0 and tile % ATTN_BLOCK == 0
    if not latent:
        assert n_tiles == 1

    in_specs = []
    args = []
    if halo:
        hb = tile // halo
        n_hblk = seq_len // halo
        in_specs.append(pl.BlockSpec((1, halo, D_MODEL), lambda b, t: (b, jnp.maximum(t * hb - 1, 0), 0)))
        args.append(x)
    in_specs.append(pl.BlockSpec((1, tile, D_MODEL), lambda b, t: (b, t, 0)))
    args.append(x)
    if halo:
        in_specs.append(pl.BlockSpec((1, halo, D_MODEL),
                                     lambda b, t: (b, jnp.minimum((t + 1) * hb, n_hblk - 1), 0)))
        args.append(x)

    if latent:
        in_specs.append(pl.BlockSpec((1, 1, N_MOD * D_MODEL), lambda b, t: (layer * MOD_ROWS + b, 0, 0)))
    else:
        in_specs.append(pl.BlockSpec((1, 1, N_MOD * D_MODEL), lambda b, t: (layer * MOD_ROWS + 8, 0, 0)))
    args.append(mod_all)

    def add_const(arr, block):
        idx = (layer,) + (0,) * (len(block) - 1)
        in_specs.append(_const_spec(block, idx))
        args.append(arr)

    add_const(params['norm1'], (1, 1, D_MODEL))
    add_const(params['norm2'], (1, 1, D_MODEL))
    add_const(params['w_in'], (1, D_MODEL, IN_W))
    add_const(params['w_out'], (1, D_MODEL, D_MODEL))
    in_specs.append(pl.BlockSpec(memory_space=pltpu.SMEM))
    args.append(params['attn_sink'])
    add_const(params['pool_w'], (1, POOL_W, POOL_W))
    add_const(params['pool_scale'], (1, 1, POOL_W))
    add_const(params['conv_dw'], (1, CONV_WIDTH, CONV_W))
    add_const(params['conv_b'], (1, 1, CONV_W))
    add_const(params['conv_norm'], (1, 1, CONV_W))
    add_const(params['conv_pw'], (1, CONV_W, CONV_W))
    add_const(params['gm_norm'], (1, 1, GM_W))
    add_const(params['gm_ws'], (1, GM_GROUPS, GM_CHUNK, GM_CHUNK))
    add_const(params['gm_b'], (1, GM_CHUNK, GM_W))
    add_const(params['w_mlp1'], (1, D_MODEL, D_FF))
    add_const(params['w_mlp2'], (1, D_FF, D_MODEL))
    if latent:
        cos_t, sin_t = rope
        in_specs.append(_const_spec(cos_t.shape, (0, 0)))
        args.append(cos_t)
        in_specs.append(_const_spec(sin_t.shape, (0, 0)))
        args.append(sin_t)
        ck, cv = cache
        past = ck.shape[2]
        in_specs.append(pl.BlockSpec((1, 1, past, KV_W), lambda b, t: (b, layer, 0, 0)))
        args.append(ck)
        in_specs.append(pl.BlockSpec((1, 1, past, KV_W), lambda b, t: (b, layer, 0, 0)))
        args.append(cv)
    if last:
        in_specs.append(_const_spec((1, D_MODEL), (0, 0)))
        args.append(final_norm)

    out_shape = [jax.ShapeDtypeStruct(x.shape, jnp.float32)]
    out_specs = [pl.BlockSpec((1, tile, D_MODEL), lambda b, t: (b, t, 0))]
    if not latent:
        for _ in range(2):
            out_shape.append(jax.ShapeDtypeStruct((batch, seq_len, KV_W), jnp.float32))
            out_specs.append(pl.BlockSpec((1, tile, KV_W), lambda b, t: (b, t, 0)))

    scratch = [
        pltpu.VMEM((rows, D_MODEL), jnp.bfloat16),
        pltpu.VMEM((rows, IN_W), jnp.float32),
        pltpu.VMEM((N_Q_HEADS, tile, HEAD_DIM), jnp.bfloat16),
        pltpu.VMEM((N_KV_HEADS, rows, HEAD_DIM), jnp.bfloat16),
        pltpu.VMEM((N_KV_HEADS, rows, HEAD_DIM), jnp.bfloat16),
        pltpu.VMEM((tile + 2 * MARGIN, 256), jnp.float32),
        pltpu.VMEM((tile, D_MODEL), jnp.bfloat16),
        pltpu.VMEM((tile, D_MODEL), jnp.float32),
    ]
    kern = functools.partial(_layer_kernel, layer=layer, tile=tile, halo=halo, seq_len=seq_len, latent=latent, last=last)
    outs = pl.pallas_call(
        kern,
        grid=(batch, n_tiles),
        in_specs=in_specs,
        out_specs=out_specs,
        out_shape=out_shape,
        scratch_shapes=scratch,
        compiler_params=pltpu.CompilerParams(
            dimension_semantics=("arbitrary", "arbitrary"),
            vmem_limit_bytes=VMEM_LIMIT_BYTES),
        name=("latent" if latent else "context") + "_layer",
    )(*args)
    return outs


def _rope_tables(seq_len):
    rows = seq_len // GRID_W
    row = jnp.repeat(jnp.arange(rows), GRID_W).astype(jnp.float32)
    col = jnp.tile(jnp.arange(GRID_W), rows).astype(jnp.float32)
    inv = ROPE_BASE ** (-jnp.arange(ROPE_PAIRS, dtype=jnp.float32) / ROPE_PAIRS)
    ang = jnp.concatenate([row[:, None] * inv, col[:, None] * inv], axis=-1)
    cos, sin = jnp.cos(ang), jnp.sin(ang)
    cos_t = jnp.tile(jnp.concatenate([cos, cos], axis=-1), (1, 2))
    sin_t = jnp.tile(jnp.concatenate([-sin, sin], axis=-1), (1, 2))
    pad = ((HALO, HALO), (0, 0))
    return jnp.pad(cos_t, pad), jnp.pad(sin_t, pad)


def _prep_params(norm1, norm2, w_in, w_out, attn_sink, pool_w, pool_scale, conv_dw, conv_b, conv_norm,
                 conv_pw, gm_norm, gm_ws, gm_b, w_mlp1, w_mlp2):
    depth = w_in.shape[0]
    bf = jnp.bfloat16
    gw = POOL_W // POOL_GROUPS
    pool_bd = jnp.zeros((depth, POOL_W, POOL_W), jnp.float32)
    for g in range(POOL_GROUPS):
        pool_bd = pool_bd.at[:, g * gw:(g + 1) * gw, g * gw:(g + 1) * gw].set(pool_w[:, g])
    gm_b_full = jnp.repeat(jnp.transpose(gm_b, (0, 2, 1)), GM_GROUP_W, axis=-1)
    return {
        'norm1': norm1.reshape(depth, 1, D_MODEL),
        'norm2': norm2.reshape(depth, 1, D_MODEL),
        'w_in': w_in.astype(bf),
        'w_out': w_out.astype(bf),
        'attn_sink': attn_sink,
        'pool_w': pool_bd.astype(bf),
        'pool_scale': pool_scale.reshape(depth, 1, POOL_W),
        'conv_dw': conv_dw,
        'conv_b': conv_b.reshape(depth, 1, CONV_W),
        'conv_norm': conv_norm.reshape(depth, 1, CONV_W),
        'conv_pw': conv_pw.astype(bf),
        'gm_norm': gm_norm.reshape(depth, 1, GM_W),
        'gm_ws': gm_ws.astype(bf),
        'gm_b': gm_b_full,
        'w_mlp1': w_mlp1.astype(bf),
        'w_mlp2': w_mlp2.astype(bf),
    }


def kernel(x_prompt, x_sample, cache_k, cache_v, c, c_ctx, w_ada, b_ada, norm1, norm2, w_in, w_out, attn_sink,
           pool_w, pool_scale, conv_dw, conv_b, conv_norm, conv_pw, gm_norm, gm_ws, gm_b, w_mlp1, w_mlp2,
           final_norm):
    depth = w_in.shape[0]
    dec_batch, dec_seq, _ = x_sample.shape
    batch, seq, _ = x_prompt.shape
    past = cache_k.shape[2]

    cond = jnp.zeros((MOD_ROWS, D_MODEL), jnp.float32)
    cond = cond.at[:dec_batch].set(c).at[8].set(c_ctx)
    mod_all = _ada_mod(cond, w_ada, b_ada).reshape(depth * MOD_ROWS, 1, N_MOD * D_MODEL)

    params = _prep_params(norm1, norm2, w_in, w_out, attn_sink, pool_w, pool_scale, conv_dw, conv_b,
                          conv_norm, conv_pw, gm_norm, gm_ws, gm_b, w_mlp1, w_mlp2)
    rope = _rope_tables(dec_seq)
    ck = cache_k.reshape(dec_batch, depth, past, KV_W)
    cv = cache_v.reshape(dec_batch, depth, past, KV_W)
    fnorm = final_norm.reshape(1, D_MODEL)

    xc, xs = x_prompt, x_sample
    ks_out, vs_out = [], []
    for l in range(depth):
        last = l == depth - 1
        xc, k_l, v_l = _layer_call(xc, l, mod_all, params, tile=seq, latent=False, last=last,
                                   final_norm=fnorm if last else None)
        ks_out.append(k_l)
        vs_out.append(v_l)
        (xs,) = _layer_call(xs, l, mod_all, params, tile=min(512, dec_seq), latent=True, last=last,
                            rope=rope, cache=(ck, cv), final_norm=fnorm if last else None)
    new_k = jnp.stack(ks_out, axis=1).reshape(batch, depth, seq, N_KV_HEADS, HEAD_DIM)
    new_v = jnp.stack(vs_out, axis=1).reshape(batch, depth, seq, N_KV_HEADS, HEAD_DIM)
    return (xc, xs, new_k, new_v)
```

```python
import functools

import jax
import jax.numpy as jnp
from jax import lax
from jax.experimental import pallas as pl
from jax.experimental.pallas import tpu as pltpu

D_MODEL = 1024
DEPTH = 4
ATTN_W = 256
POOL_W = 256
CONV_W = 256
GM_W = 256
HEAD_DIM = 64
N_Q_HEADS = 4
N_KV_HEADS = 2
Q_PER_KV = 2
KV_W = 128
ATTN_BLOCK = 128
GRID_W = 64
ROPE_BASE = 10000.0
ROPE_PAIRS = 16
POOL_GROUPS = 4
POOL_SIZES = (2, 4, 8, 16)
CONV_WIDTH = 31
CONV_HALF = 15
GM_GROUPS = 4
GM_GROUP_W = 64
GM_CHUNK = 128
D_FF = 4096
N_MOD = 6
EPS = 1e-6
NEG_INF = -1e30
LOG2_E = 1.4426950408889634
IN_W = 1792
C_Q, C_K, C_V, C_POOL, C_CONV, C_GM = 0, 256, 384, 512, 768, 1280

HALO = 128
MARGIN = 16
MOD_ROWS = 16
FF_CHUNK = 1024
CONV_ROWS = 64
VMEM_LIMIT_BYTES = 58 * 1024 * 1024


def _rms(x, g):
    ms = jnp.mean(x * x, axis=-1, keepdims=True)
    return x * lax.rsqrt(ms + EPS) * g


def _dot(a, b):
    return jnp.dot(a, b, preferred_element_type=jnp.float32)


def _dot_nt(a, b):
    return lax.dot_general(a, b, (((1,), (1,)), ((), ())), preferred_element_type=jnp.float32)


def _ada_kernel(cond_ref, w_ref, b_ref, o_ref):
    s = cond_ref[...]
    s = s * jax.nn.sigmoid(s)
    o_ref[0] = _dot(s.astype(jnp.bfloat16), w_ref[0].astype(jnp.bfloat16)) + b_ref[0]


def _ada_mod(cond, w_ada, b_ada):
    tn = 1536
    nt = (N_MOD * D_MODEL) // tn
    return pl.pallas_call(
        _ada_kernel,
        grid=(DEPTH, nt),
        in_specs=[
            pl.BlockSpec((MOD_ROWS, D_MODEL), lambda l, j: (0, 0)),
            pl.BlockSpec((1, D_MODEL, tn), lambda l, j: (l, 0, j)),
            pl.BlockSpec((1, 1, tn), lambda l, j: (l, 0, j)),
        ],
        out_specs=pl.BlockSpec((1, MOD_ROWS, tn), lambda l, j: (l, 0, j)),
        out_shape=jax.ShapeDtypeStruct((DEPTH, MOD_ROWS, N_MOD * D_MODEL), jnp.float32),
        compiler_params=pltpu.CompilerParams(
            dimension_semantics=("arbitrary", "arbitrary"),
            vmem_limit_bytes=VMEM_LIMIT_BYTES),
        name="ada_mod",
    )(cond, w_ada, b_ada.reshape(DEPTH, 1, N_MOD * D_MODEL))


def _layer_kernel(*refs, layer, tile, halo, seq_len, latent, last):
    it = iter(refs)
    if halo:
        xprev_ref = next(it)
    x_ref = next(it)
    if halo:
        xnext_ref = next(it)
    mod_ref = next(it)
    norm1_ref = next(it)
    norm2_ref = next(it)
    w_in_ref = next(it)
    w_out_ref = next(it)
    sink_ref = next(it)
    pool_w_ref = next(it)
    pool_scale_ref = next(it)
    conv_dw_ref = next(it)
    conv_b_ref = next(it)
    conv_norm_ref = next(it)
    conv_pw_ref = next(it)
    gm_norm_ref = next(it)
    gm_ws_ref = next(it)
    gm_b_ref = next(it)
    w1_ref = next(it)
    w2_ref = next(it)
    if latent:
        cos_ref = next(it)
        sin_ref = next(it)
        ck_ref = next(it)
        cv_ref = next(it)
    if last:
        fnorm_ref = next(it)
    out_ref = next(it)
    if not latent:
        kout_ref = next(it)
        vout_ref = next(it)
    h_scr = next(it)
    proj_scr = next(it)
    q_scr = next(it)
    k_scr = next(it)
    v_scr = next(it)
    slab_scr = next(it)
    mix_scr = next(it)
    acc_scr = next(it)

    rows = tile + 2 * halo
    ti = pl.program_id(1)
    n_tiles = seq_len // tile
    is_first = ti == 0
    is_last_tile = ti == n_tiles - 1

    mod = mod_ref[0]
    sh1 = mod[:, 0 * D_MODEL:1 * D_MODEL]
    sc1 = mod[:, 1 * D_MODEL:2 * D_MODEL]
    g1 = mod[:, 2 * D_MODEL:3 * D_MODEL]
    sh2 = mod[:, 3 * D_MODEL:4 * D_MODEL]
    sc2 = mod[:, 4 * D_MODEL:5 * D_MODEL]
    g2 = mod[:, 5 * D_MODEL:6 * D_MODEL]
    norm1 = norm1_ref[0]
    norm2 = norm2_ref[0]

    def norm_mod1(x):
        return (_rms(x, norm1) * (1.0 + sc1) + sh1).astype(jnp.bfloat16)

    if halo:
        h_scr[0:halo, :] = norm_mod1(xprev_ref[0])
        h_scr[halo + tile:rows, :] = norm_mod1(xnext_ref[0])
    h_scr[halo:halo + tile, :] = norm_mod1(x_ref[0])
    c0 = halo
    c1 = halo + tile
    if halo:
        m0, m1 = c0 - MARGIN, c1 + MARGIN
        proj_scr[c0:c1, C_Q:C_K] = _dot(h_scr[c0:c1, :], w_in_ref[0, :, C_Q:C_K])
        proj_scr[:, C_K:C_POOL] = _dot(h_scr[...], w_in_ref[0, :, C_K:C_POOL])
        proj_scr[m0:m1, C_POOL:C_GM] = _dot(h_scr[m0:m1, :], w_in_ref[0, :, C_POOL:C_GM])
        proj_scr[c0:c1, C_GM:IN_W] = _dot(h_scr[c0:c1, :], w_in_ref[0, :, C_GM:IN_W])
    else:
        proj_scr[...] = _dot(h_scr[...], w_in_ref[0])

    scale = HEAD_DIM ** -0.5 * LOG2_E
    if latent:
        row0 = ti * tile
        cos_t = cos_ref[pl.ds(pl.multiple_of(row0, tile), rows), :]
        sin_t = sin_ref[pl.ds(pl.multiple_of(row0, tile), rows), :]
        lane = lax.broadcasted_iota(jnp.int32, (1, 128), 1)
        first_half = (lane % HEAD_DIM) < (HEAD_DIM // 2)

        def rope(z, cs, sn):
            swapped = jnp.where(first_half, pltpu.roll(z, 96, 1), pltpu.roll(z, 32, 1))
            return z * cs + swapped * sn

    k_all = proj_scr[:, C_K:C_K + KV_W]
    if not latent:
        kout_ref[0] = k_all
        vout_ref[0] = proj_scr[:, C_V:C_V + KV_W]
    if latent:
        k_all = rope(k_all, cos_t, sin_t)
    k_bf = k_all.astype(jnp.bfloat16)
    v_bf = proj_scr[:, C_V:C_V + KV_W].astype(jnp.bfloat16)
    for g in range(N_KV_HEADS):
        k_scr[g] = k_bf[:, g * HEAD_DIM:(g + 1) * HEAD_DIM]
        v_scr[g] = v_bf[:, g * HEAD_DIM:(g + 1) * HEAD_DIM]
    for half in range(2):
        qh = proj_scr[c0:c1, C_Q + half * 128:C_Q + (half + 1) * 128]
        if latent:
            qh = rope(qh, cos_t[c0:c1], sin_t[c0:c1])
        qh = (qh * scale).astype(jnp.bfloat16)
        for s in range(2):
            q_scr[2 * half + s] = qh[:, s * HEAD_DIM:(s + 1) * HEAD_DIM]

    if latent:
        ck_bf = ck_ref[0, 0].astype(jnp.bfloat16)
        cv_bf = cv_ref[0, 0].astype(jnp.bfloat16)
        r_i = lax.broadcasted_iota(jnp.int32, (2 * ATTN_BLOCK, 3 * ATTN_BLOCK), 0) % ATTN_BLOCK
        c_i = lax.broadcasted_iota(jnp.int32, (2 * ATTN_BLOCK, 3 * ATTN_BLOCK), 1)
        band_ok = jnp.abs(r_i - (c_i - ATTN_BLOCK)) <= HALO
        n_qblk = tile // ATTN_BLOCK
        head_row = lax.broadcasted_iota(jnp.int32, (2 * ATTN_BLOCK, 1), 0) < ATTN_BLOCK
        for j in range(n_qblk):
            bias = jnp.where(band_ok, 0.0, NEG_INF)
            if j == 0:
                lo = jnp.where(is_first, ATTN_BLOCK, 0)
                bias = jnp.where(c_i >= lo, bias, NEG_INF)
            if j == n_qblk - 1:
                hi = jnp.where(is_last_tile, 2 * ATTN_BLOCK, 3 * ATTN_BLOCK)
                bias = jnp.where(c_i < hi, bias, NEG_INF)
            js = slice(j * ATTN_BLOCK, (j + 1) * ATTN_BLOCK)
            ks = slice(j * ATTN_BLOCK, (j + 3) * ATTN_BLOCK)
            for g in range(N_KV_HEADS):
                q2 = jnp.concatenate([q_scr[2 * g, js, :], q_scr[2 * g + 1, js, :]], axis=0)
                s_loc = _dot_nt(q2, k_scr[g, ks, :]) + bias
                s_ctx = _dot_nt(q2, ck_bf[:, g * HEAD_DIM:(g + 1) * HEAD_DIM])
                sink = jnp.where(head_row, sink_ref[layer, 2 * g] * LOG2_E, sink_ref[layer, 2 * g + 1] * LOG2_E)
                m = jnp.maximum(jnp.maximum(jnp.max(s_loc, axis=-1, keepdims=True),
                                            jnp.max(s_ctx, axis=-1, keepdims=True)), sink)
                p_loc = jnp.exp2(s_loc - m)
                p_ctx = jnp.exp2(s_ctx - m)
                den = (jnp.sum(p_loc, axis=-1, keepdims=True) + jnp.sum(p_ctx, axis=-1, keepdims=True)
                       + jnp.exp2(sink - m))
                o = (_dot(p_loc.astype(jnp.bfloat16), v_scr[g, ks, :])
                     + _dot(p_ctx.astype(jnp.bfloat16), cv_bf[:, g * HEAD_DIM:(g + 1) * HEAD_DIM]))
                o = (o / den).astype(jnp.bfloat16)
                mix_scr[js, (2 * g) * HEAD_DIM:(2 * g + 1) * HEAD_DIM] = o[:ATTN_BLOCK]
                mix_scr[js, (2 * g + 1) * HEAD_DIM:(2 * g + 2) * HEAD_DIM] = o[ATTN_BLOCK:]
    else:
        head_row = lax.broadcasted_iota(jnp.int32, (2 * tile, 1), 0) < tile
        for g in range(N_KV_HEADS):
            q2 = jnp.concatenate([q_scr[2 * g], q_scr[2 * g + 1]], axis=0)
            s = _dot_nt(q2, k_scr[g])
            sink = jnp.where(head_row, sink_ref[layer, 2 * g] * LOG2_E, sink_ref[layer, 2 * g + 1] * LOG2_E)
            m = jnp.maximum(jnp.max(s, axis=-1, keepdims=True), sink)
            p = jnp.exp2(s - m)
            den = jnp.sum(p, axis=-1, keepdims=True) + jnp.exp2(sink - m)
            o = (_dot(p.astype(jnp.bfloat16), v_scr[g]) / den).astype(jnp.bfloat16)
            mix_scr[:, (2 * g) * HEAD_DIM:(2 * g + 1) * HEAD_DIM] = o[:tile]
            mix_scr[:, (2 * g + 1) * HEAD_DIM:(2 * g + 2) * HEAD_DIM] = o[tile:]

    srows = tile + 2 * MARGIN

    def fill_slab(fn):
        slab_scr[MARGIN:MARGIN + tile, :] = fn(c0, c1)
        zeros = jnp.zeros((MARGIN, 256), jnp.float32)
        if halo:
            left = fn(c0 - MARGIN, c0)
            right = fn(c1, c1 + MARGIN)
            slab_scr[0:MARGIN, :] = jnp.where(is_first, zeros, left)
            slab_scr[MARGIN + tile:srows, :] = jnp.where(is_last_tile, zeros, right)
        else:
            slab_scr[0:MARGIN, :] = zeros
            slab_scr[MARGIN + tile:srows, :] = zeros

    fill_slab(lambda a, b: proj_scr[a:b, C_POOL:C_POOL + POOL_W])
    xs = slab_scr[...]
    sums = []
    acc = xs
    step = 1
    for size in POOL_SIZES:
        acc = acc + pltpu.roll(acc, step, 0)
        step = size
        shift = size // 2 - 1
        win = acc if shift == 0 else pltpu.roll(acc, srows - shift, 0)
        sums.append(win[MARGIN:MARGIN + tile])
    lane256 = lax.broadcasted_iota(jnp.int32, (1, POOL_W), 1)
    grp = lane256 // (POOL_W // POOL_GROUPS)
    wsum = jnp.where(grp == 0, sums[0], jnp.where(grp == 1, sums[1], jnp.where(grp == 2, sums[2], sums[3])))
    half_w = jnp.where(grp == 0, 1, jnp.where(grp == 1, 2, jnp.where(grp == 2, 4, 8)))
    tpos = ti * tile + lax.broadcasted_iota(jnp.int32, (tile, 1), 0)
    cnt = (jnp.minimum(tpos + half_w, seq_len) - jnp.maximum(tpos - half_w, 0)).astype(jnp.float32)
    pooled = wsum / cnt - xs[MARGIN:MARGIN + tile]
    pool_out = _dot(pooled.astype(jnp.bfloat16), pool_w_ref[0]) * pool_scale_ref[0]
    mix_scr[:, ATTN_W:ATTN_W + POOL_W] = pool_out.astype(jnp.bfloat16)

    def glu(a, b):
        av = proj_scr[a:b, C_CONV:C_CONV + CONV_W]
        gv = proj_scr[a:b, C_CONV + CONV_W:C_CONV + 2 * CONV_W]
        return av * jax.nn.sigmoid(gv)

    fill_slab(glu)
    conv_cols = slice(ATTN_W + POOL_W, ATTN_W + POOL_W + CONV_W)
    first_tap = MARGIN - CONV_HALF
    for blk in range(tile // CONV_ROWS):
        b0 = blk * CONV_ROWS
        y = None
        for r in range(8):
            z = None
            for a in range((first_tap + CONV_WIDTH + 7) // 8):
                o = 8 * a + r
                if first_tap <= o < first_tap + CONV_WIDTH:
                    j = o - first_tap
                    term = slab_scr[b0 + 8 * a:b0 + 8 * a + CONV_ROWS + 8, :] * conv_dw_ref[0, j:j + 1, :]
                    z = term if z is None else z + term
            zr = z[r:r + CONV_ROWS]
            y = zr if y is None else y + zr
        y = _rms(y + conv_b_ref[0], conv_norm_ref[0])
        y = y * jax.nn.sigmoid(y)
        mix_scr[b0:b0 + CONV_ROWS, conv_cols] = y.astype(jnp.bfloat16)
    conv_out = _dot(mix_scr[:, conv_cols], conv_pw_ref[0])
    mix_scr[:, conv_cols] = conv_out.astype(jnp.bfloat16)

    gu = jax.nn.gelu(proj_scr[c0:c1, C_GM:C_GM + GM_W])
    gv = jax.nn.gelu(proj_scr[c0:c1, C_GM + GM_W:C_GM + 2 * GM_W])
    gv = _rms(gv, gm_norm_ref[0]).astype(jnp.bfloat16)
    gm_b = gm_b_ref[0]
    for c in range(tile // GM_CHUNK):
        cs = slice(c * GM_CHUNK, (c + 1) * GM_CHUNK)
        parts = []
        for g in range(GM_GROUPS):
            parts.append(_dot(gm_ws_ref[0, g], gv[cs, g * GM_GROUP_W:(g + 1) * GM_GROUP_W]))
        sv = jnp.concatenate(parts, axis=-1) + gm_b
        mix_scr[cs, ATTN_W + POOL_W + CONV_W:D_MODEL] = (gu[cs] * sv).astype(jnp.bfloat16)

    x1 = x_ref[0] + g1 * _dot(mix_scr[...], w_out_ref[0])
    out_ref[0] = x1
    h_scr[0:tile, :] = (_rms(x1, norm2) * (1.0 + sc2) + sh2).astype(jnp.bfloat16)

    for c in range(D_FF // FF_CHUNK):
        a = _dot(h_scr[0:tile, :], w1_ref[0, :, c * FF_CHUNK:(c + 1) * FF_CHUNK])
        a = jnp.square(jnp.maximum(a, 0.0)).astype(jnp.bfloat16)
        f = _dot(a, w2_ref[0, c * FF_CHUNK:(c + 1) * FF_CHUNK, :])
        if c == 0:
            acc_scr[...] = f
        else:
            acc_scr[...] += f
    x2 = out_ref[0] + g2 * acc_scr[...]
    if last:
        x2 = _rms(x2, fnorm_ref[...])
    out_ref[0] = x2


def _const_spec(shape, index):
    return pl.BlockSpec(shape, lambda b, t: index, pipeline_mode=pl.Buffered(1))


def _layer_call(x, layer, mod_all, params, *, tile, latent, last, rope=None, cache=None, final_norm=None):
    batch, seq_len, _ = x.shape
    halo = HALO if latent else 0
    rows = tile + 2 * halo
    n_tiles = seq_len // tile
    assert seq_len % tile == 0 and tile % ATTN_BLOCK == 0
    if not latent:
        assert n_tiles == 1

    in_specs = []
    args = []
    if halo:
        hb = tile // halo
        n_hblk = seq_len // halo
        in_specs.append(pl.BlockSpec((1, halo, D_MODEL), lambda b, t: (b, jnp.maximum(t * hb - 1, 0), 0)))
        args.append(x)
    in_specs.append(pl.BlockSpec((1, tile, D_MODEL), lambda b, t: (b, t, 0)))
    args.append(x)
    if halo:
        in_specs.append(pl.BlockSpec((1, halo, D_MODEL),
                                     lambda b, t: (b, jnp.minimum((t + 1) * hb, n_hblk - 1), 0)))
        args.append(x)

    if latent:
        in_specs.append(pl.BlockSpec((1, 1, N_MOD * D_MODEL), lambda b, t: (layer * MOD_ROWS + b, 0, 0)))
    else:
        in_specs.append(pl.BlockSpec((1, 1, N_MOD * D_MODEL), lambda b, t: (layer * MOD_ROWS + 8, 0, 0)))
    args.append(mod_all)

    def add_const(arr, block):
        idx = (layer,) + (0,) * (len(block) - 1)
        in_specs.append(_const_spec(block, idx))
        args.append(arr)

    add_const(params['norm1'], (1, 1, D_MODEL))
    add_const(params['norm2'], (1, 1, D_MODEL))
    add_const(params['w_in'], (1, D_MODEL, IN_W))
    add_const(params['w_out'], (1, D_MODEL, D_MODEL))
    in_specs.append(pl.BlockSpec(memory_space=pltpu.SMEM))
    args.append(params['attn_sink'])
    add_const(params['pool_w'], (1, POOL_W, POOL_W))
    add_const(params['pool_scale'], (1, 1, POOL_W))
    add_const(params['conv_dw'], (1, CONV_WIDTH, CONV_W))
    add_const(params['conv_b'], (1, 1, CONV_W))
    add_const(params['conv_norm'], (1, 1, CONV_W))
    add_const(params['conv_pw'], (1, CONV_W, CONV_W))
    add_const(params['gm_norm'], (1, 1, GM_W))
    add_const(params['gm_ws'], (1, GM_GROUPS, GM_CHUNK, GM_CHUNK))
    add_const(params['gm_b'], (1, GM_CHUNK, GM_W))
    add_const(params['w_mlp1'], (1, D_MODEL, D_FF))
    add_const(params['w_mlp2'], (1, D_FF, D_MODEL))
    if latent:
        cos_t, sin_t = rope
        in_specs.append(_const_spec(cos_t.shape, (0, 0)))
        args.append(cos_t)
        in_specs.append(_const_spec(sin_t.shape, (0, 0)))
        args.append(sin_t)
        ck, cv = cache
        past = ck.shape[2]
        in_specs.append(pl.BlockSpec((1, 1, past, KV_W), lambda b, t: (b, layer, 0, 0)))
        args.append(ck)
        in_specs.append(pl.BlockSpec((1, 1, past, KV_W), lambda b, t: (b, layer, 0, 0)))
        args.append(cv)
    if last:
        in_specs.append(_const_spec((1, D_MODEL), (0, 0)))
        args.append(final_norm)

    out_shape = [jax.ShapeDtypeStruct(x.shape, jnp.float32)]
    out_specs = [pl.BlockSpec((1, tile, D_MODEL), lambda b, t: (b, t, 0))]
    if not latent:
        for _ in range(2):
            out_shape.append(jax.ShapeDtypeStruct((batch, seq_len, KV_W), jnp.float32))
            out_specs.append(pl.BlockSpec((1, tile, KV_W), lambda b, t: (b, t, 0)))

    scratch = [
        pltpu.VMEM((rows, D_MODEL), jnp.bfloat16),
        pltpu.VMEM((rows, IN_W), jnp.float32),
        pltpu.VMEM((N_Q_HEADS, tile, HEAD_DIM), jnp.bfloat16),
        pltpu.VMEM((N_KV_HEADS, rows, HEAD_DIM), jnp.bfloat16),
        pltpu.VMEM((N_KV_HEADS, rows, HEAD_DIM), jnp.bfloat16),
        pltpu.VMEM((tile + 2 * MARGIN, 256), jnp.float32),
        pltpu.VMEM((tile, D_MODEL), jnp.bfloat16),
        pltpu.VMEM((tile, D_MODEL), jnp.float32),
    ]
    kern = functools.partial(_layer_kernel, layer=layer, tile=tile, halo=halo, seq_len=seq_len, latent=latent, last=last)
    outs = pl.pallas_call(
        kern,
        grid=(batch, n_tiles),
        in_specs=in_specs,
        out_specs=out_specs,
        out_shape=out_shape,
        scratch_shapes=scratch,
        compiler_params=pltpu.CompilerParams(
            dimension_semantics=("arbitrary", "arbitrary"),
            vmem_limit_bytes=VMEM_LIMIT_BYTES),
        name=("latent" if latent else "context") + "_layer",
    )(*args)
    return outs


def _rope_tables(seq_len):
    rows = seq_len // GRID_W
    row = jnp.repeat(jnp.arange(rows), GRID_W).astype(jnp.float32)
    col = jnp.tile(jnp.arange(GRID_W), rows).astype(jnp.float32)
    inv = ROPE_BASE ** (-jnp.arange(ROPE_PAIRS, dtype=jnp.float32) / ROPE_PAIRS)
    ang = jnp.concatenate([row[:, None] * inv, col[:, None] * inv], axis=-1)
    cos, sin = jnp.cos(ang), jnp.sin(ang)
    cos_t = jnp.tile(jnp.concatenate([cos, cos], axis=-1), (1, 2))
    sin_t = jnp.tile(jnp.concatenate([-sin, sin], axis=-1), (1, 2))
    pad = ((HALO, HALO), (0, 0))
    return jnp.pad(cos_t, pad), jnp.pad(sin_t, pad)


def _prep_params(norm1, norm2, w_in, w_out, attn_sink, pool_w, pool_scale, conv_dw, conv_b, conv_norm,
                 conv_pw, gm_norm, gm_ws, gm_b, w_mlp1, w_mlp2):
    depth = w_in.shape[0]
    bf = jnp.bfloat16
    gw = POOL_W // POOL_GROUPS
    pool_bd = jnp.zeros((depth, POOL_W, POOL_W), jnp.float32)
    for g in range(POOL_GROUPS):
        pool_bd = pool_bd.at[:, g * gw:(g + 1) * gw, g * gw:(g + 1) * gw].set(pool_w[:, g])
    gm_b_full = jnp.repeat(jnp.transpose(gm_b, (0, 2, 1)), GM_GROUP_W, axis=-1)
    return {
        'norm1': norm1.reshape(depth, 1, D_MODEL),
        'norm2': norm2.reshape(depth, 1, D_MODEL),
        'w_in': w_in.astype(bf),
        'w_out': w_out.astype(bf),
        'attn_sink': attn_sink,
        'pool_w': pool_bd.astype(bf),
        'pool_scale': pool_scale.reshape(depth, 1, POOL_W),
        'conv_dw': conv_dw,
        'conv_b': conv_b.reshape(depth, 1, CONV_W),
        'conv_norm': conv_norm.reshape(depth, 1, CONV_W),
        'conv_pw': conv_pw.astype(bf),
        'gm_norm': gm_norm.reshape(depth, 1, GM_W),
        'gm_ws': gm_ws.astype(bf),
        'gm_b': gm_b_full,
        'w_mlp1': w_mlp1.astype(bf),
        'w_mlp2': w_mlp2.astype(bf),
    }


def kernel(x_prompt, x_sample, cache_k, cache_v, c, c_ctx, w_ada, b_ada, norm1, norm2, w_in, w_out, attn_sink,
           pool_w, pool_scale, conv_dw, conv_b, conv_norm, conv_pw, gm_norm, gm_ws, gm_b, w_mlp1, w_mlp2,
           final_norm):
    depth = w_in.shape[0]
    dec_batch, dec_seq, _ = x_sample.shape
    batch, seq, _ = x_prompt.shape
    past = cache_k.shape[2]

    cond = jnp.zeros((MOD_ROWS, D_MODEL), jnp.float32)
    cond = cond.at[:dec_batch].set(c).at[8].set(c_ctx)
    mod_all = _ada_mod(cond, w_ada, b_ada).reshape(depth * MOD_ROWS, 1, N_MOD * D_MODEL)

    params = _prep_params(norm1, norm2, w_in, w_out, attn_sink, pool_w, pool_scale, conv_dw, conv_b,
                          conv_norm, conv_pw, gm_norm, gm_ws, gm_b, w_mlp1, w_mlp2)
    rope = _rope_tables(dec_seq)
    ck = cache_k.reshape(dec_batch, depth, past, KV_W)
    cv = cache_v.reshape(dec_batch, depth, past, KV_W)
    fnorm = final_norm.reshape(1, D_MODEL)

    xc, xs = x_prompt, x_sample
    ks_out, vs_out = [], []
    for l in range(depth):
        last = l == depth - 1
        xc, k_l, v_l = _layer_call(xc, l, mod_all, params, tile=seq, latent=False, last=last,
                                   final_norm=fnorm if last else None)
        ks_out.append(k_l)
        vs_out.append(v_l)
        (xs,) = _layer_call(xs, l, mod_all, params, tile=min(512, dec_seq), latent=True, last=last,
                            rope=rope, cache=(ck, cv), final_norm=fnorm if last else None)
    new_k = jnp.stack(ks_out, axis=1).reshape(batch, depth, seq, N_KV_HEADS, HEAD_DIM)
    new_v = jnp.stack(vs_out, axis=1).reshape(batch, depth, seq, N_KV_HEADS, HEAD_DIM)
    return (xc, xs, new_k, new_v)
```

```python
import functools
import types

import jax
import jax.numpy as jnp
from jax import lax
from jax.experimental import pallas as pl
from jax.experimental.pallas import tpu as pltpu

D_MODEL = 1024
DEPTH = 4
ATTN_W = 256
POOL_W = 256
CONV_W = 256
GM_W = 256
HEAD_DIM = 64
N_Q_HEADS = 4
N_KV_HEADS = 2
Q_PER_KV = 2
KV_W = 128
ATTN_BLOCK = 128
GRID_W = 64
ROPE_BASE = 10000.0
ROPE_PAIRS = 16
POOL_GROUPS = 4
POOL_SIZES = (2, 4, 8, 16)
CONV_WIDTH = 31
CONV_HALF = 15
GM_GROUPS = 4
GM_GROUP_W = 64
GM_CHUNK = 128
D_FF = 4096
N_MOD = 6
EPS = 1e-6
NEG_INF = -1e30
LOG2_E = 1.4426950408889634
IN_W = 1792
C_Q, C_K, C_V, C_POOL, C_CONV, C_GM = 0, 256, 384, 512, 768, 1280

HALO = 128
MARGIN = 16
MOD_ROWS = 16
CTX_MOD_ROW = 8
FF_CHUNK = 512
CONV_ROWS = 64
STEP_ROWS = 512
VMEM_LIMIT_BYTES = 58 * 1024 * 1024


def _rms(x, g):
    ms = jnp.mean(x * x, axis=-1, keepdims=True)
    return x * lax.rsqrt(ms + EPS) * g


def _dot(a, b):
    return jnp.dot(a, b, preferred_element_type=jnp.float32)


def _dot_nt(a, b):
    return lax.dot_general(a, b, (((1,), (1,)), ((), ())), preferred_element_type=jnp.float32)


def _ada_kernel(cond_ref, w_ref, b_ref, o_ref):
    s = cond_ref[...]
    s = s * jax.nn.sigmoid(s)
    o_ref[0] = _dot(s.astype(jnp.bfloat16), w_ref[0].astype(jnp.bfloat16)) + b_ref[0]


def _ada_mod(cond, w_ada, b_ada):
    tn = 1536
    nt = (N_MOD * D_MODEL) // tn
    return pl.pallas_call(
        _ada_kernel,
        grid=(DEPTH, nt),
        in_specs=[
            pl.BlockSpec((MOD_ROWS, D_MODEL), lambda l, j: (0, 0)),
            pl.BlockSpec((1, D_MODEL, tn), lambda l, j: (l, 0, j)),
            pl.BlockSpec((1, 1, tn), lambda l, j: (l, 0, j)),
        ],
        out_specs=pl.BlockSpec((1, MOD_ROWS, tn), lambda l, j: (l, 0, j)),
        out_shape=jax.ShapeDtypeStruct((DEPTH, MOD_ROWS, N_MOD * D_MODEL), jnp.float32),
        compiler_params=pltpu.CompilerParams(
            dimension_semantics=("arbitrary", "arbitrary"),
            vmem_limit_bytes=VMEM_LIMIT_BYTES),
        name="ada_mod",
    )(cond, w_ada, b_ada.reshape(DEPTH, 1, N_MOD * D_MODEL))


def _mlp_half(r, cfg):
    tile, n_seq = cfg.tile, cfg.n_seq
    slot = 1 - pl.program_id(0) % 2
    g2 = r.modp[0][:, 5 * D_MODEL:6 * D_MODEL]
    for c in range(D_FF // FF_CHUNK):
        a = _dot(r.h2_scr[slot], r.w1[0, :, c * FF_CHUNK:(c + 1) * FF_CHUNK])
        a = jnp.square(jnp.maximum(a, 0.0)).astype(jnp.bfloat16)
        yield
        f = _dot(a, r.w2[0, c * FF_CHUNK:(c + 1) * FF_CHUNK, :])
        if c == 0:
            r.acc_scr[...] = f
        else:
            r.acc_scr[...] += f
        yield
    x2 = r.x1_scr[slot] + g2 * r.acc_scr[...]
    if cfg.last:
        x2 = _rms(x2, r.fnorm[...])
    for q in range(n_seq):
        r.out[q] = x2[q * tile:(q + 1) * tile]


def _mixer_half(r, cfg):
    tile, n_seq, halo, seq_len, latent, layer = cfg.tile, cfg.n_seq, cfg.halo, cfg.seq_len, cfg.latent, cfg.layer
    rows = tile + 2 * halo
    step = pl.program_id(0)
    slot = step % 2
    cur = jnp.minimum(step, cfg.n_steps - 1)
    n_tiles = seq_len // tile
    ti = cur % n_tiles
    is_first = ti == 0
    is_last_tile = ti == n_tiles - 1
    h_scr, proj_scr, q_scr, k_scr, v_scr, slab_scr, mix_scr = (
        r.h_scr, r.proj_scr, r.q_scr, r.k_scr, r.v_scr, r.slab_scr, r.mix_scr)

    mod = r.mod[0]
    sh1 = mod[:, 0 * D_MODEL:1 * D_MODEL]
    sc1 = mod[:, 1 * D_MODEL:2 * D_MODEL]
    g1 = mod[:, 2 * D_MODEL:3 * D_MODEL]
    sh2 = mod[:, 3 * D_MODEL:4 * D_MODEL]
    sc2 = mod[:, 4 * D_MODEL:5 * D_MODEL]
    norm1 = r.norm1[0]
    norm2 = r.norm2[0]

    def norm_mod1(x):
        return (_rms(x, norm1) * (1.0 + sc1) + sh1).astype(jnp.bfloat16)

    c0 = halo
    c1 = halo + tile
    if halo:
        h_scr[0:halo, :] = norm_mod1(r.xprev[0])
        h_scr[c1:rows, :] = norm_mod1(r.xnext[0])
        h_scr[c0:c1, :] = norm_mod1(r.x[0])
        m0, m1 = c0 - MARGIN, c1 + MARGIN
        proj_scr[c0:c1, C_Q:C_K] = _dot(h_scr[c0:c1, :], r.w_in[0, :, C_Q:C_K])
        proj_scr[:, C_K:C_POOL] = _dot(h_scr[...], r.w_in[0, :, C_K:C_POOL])
        proj_scr[m0:m1, C_POOL:C_GM] = _dot(h_scr[m0:m1, :], r.w_in[0, :, C_POOL:C_GM])
        proj_scr[c0:c1, C_GM:IN_W] = _dot(h_scr[c0:c1, :], r.w_in[0, :, C_GM:IN_W])
    else:
        for q in range(n_seq):
            h_scr[q * tile:(q + 1) * tile, :] = norm_mod1(r.x[q])
        proj_scr[...] = _dot(h_scr[...], r.w_in[0])
    yield

    scale = HEAD_DIM ** -0.5 * LOG2_E
    if latent:
        row0 = ti * tile
        cos_t = r.cos[pl.ds(pl.multiple_of(row0, tile), rows), :]
        sin_t = r.sin[pl.ds(pl.multiple_of(row0, tile), rows), :]
        lane = lax.broadcasted_iota(jnp.int32, (1, 128), 1)
        first_half = (lane % HEAD_DIM) < (HEAD_DIM // 2)

        def rope(z, cs, sn):
            swapped = jnp.where(first_half, pltpu.roll(z, 96, 1), pltpu.roll(z, 32, 1))
            return z * cs + swapped * sn

    k_all = proj_scr[:, C_K:C_K + KV_W]
    if not latent:
        for q in range(n_seq):
            r.kout[q] = k_all[q * tile:(q + 1) * tile]
            r.vout[q] = proj_scr[q * tile:(q + 1) * tile, C_V:C_V + KV_W]
    if latent:
        k_all = rope(k_all, cos_t, sin_t)
    k_bf = k_all.astype(jnp.bfloat16)
    v_bf = proj_scr[:, C_V:C_V + KV_W].astype(jnp.bfloat16)
    for g in range(N_KV_HEADS):
        k_scr[g] = k_bf[:, g * HEAD_DIM:(g + 1) * HEAD_DIM]
        v_scr[g] = v_bf[:, g * HEAD_DIM:(g + 1) * HEAD_DIM]
    for q in range(n_seq):
        for half in range(2):
            qh = proj_scr[q * rows + c0:q * rows + c1, C_Q + half * 128:C_Q + (half + 1) * 128]
            if latent:
                qh = rope(qh, cos_t[c0:c1], sin_t[c0:c1])
            qh = (qh * scale).astype(jnp.bfloat16)
            for s in range(2):
                q_scr[2 * half + s, q * tile:(q + 1) * tile, :] = qh[:, s * HEAD_DIM:(s + 1) * HEAD_DIM]
    yield

    if latent:
        ck_bf = r.ck[0, 0].astype(jnp.bfloat16)
        cv_bf = r.cv[0, 0].astype(jnp.bfloat16)
        r_i = lax.broadcasted_iota(jnp.int32, (2 * ATTN_BLOCK, 3 * ATTN_BLOCK), 0) % ATTN_BLOCK
        c_i = lax.broadcasted_iota(jnp.int32, (2 * ATTN_BLOCK, 3 * ATTN_BLOCK), 1)
        band_ok = jnp.abs(r_i - (c_i - ATTN_BLOCK)) <= HALO
        n_qblk = tile // ATTN_BLOCK
        head_row = lax.broadcasted_iota(jnp.int32, (2 * ATTN_BLOCK, 1), 0) < ATTN_BLOCK
        for j in range(n_qblk):
            bias = jnp.where(band_ok, 0.0, NEG_INF)
            if j == 0:
                lo = jnp.where(is_first, ATTN_BLOCK, 0)
                bias = jnp.where(c_i >= lo, bias, NEG_INF)
            if j == n_qblk - 1:
                hi = jnp.where(is_last_tile, 2 * ATTN_BLOCK, 3 * ATTN_BLOCK)
                bias = jnp.where(c_i < hi, bias, NEG_INF)
            js = slice(j * ATTN_BLOCK, (j + 1) * ATTN_BLOCK)
            ks = slice(j * ATTN_BLOCK, (j + 3) * ATTN_BLOCK)
            for g in range(N_KV_HEADS):
                q2 = jnp.concatenate([q_scr[2 * g, js, :], q_scr[2 * g + 1, js, :]], axis=0)
                s_loc = _dot_nt(q2, k_scr[g, ks, :]) + bias
                s_ctx = _dot_nt(q2, ck_bf[:, g * HEAD_DIM:(g + 1) * HEAD_DIM])
                sink = jnp.where(head_row, r.sink[layer, 2 * g] * LOG2_E, r.sink[layer, 2 * g + 1] * LOG2_E)
                m = jnp.maximum(jnp.maximum(jnp.max(s_loc, axis=-1, keepdims=True),
                                            jnp.max(s_ctx, axis=-1, keepdims=True)), sink)
                yield
                p_loc = jnp.exp2(s_loc - m)
                p_ctx = jnp.exp2(s_ctx - m)
                den = (jnp.sum(p_loc, axis=-1, keepdims=True) + jnp.sum(p_ctx, axis=-1, keepdims=True)
                       + jnp.exp2(sink - m))
                yield
                o = (_dot(p_loc.astype(jnp.bfloat16), v_scr[g, ks, :])
                     + _dot(p_ctx.astype(jnp.bfloat16), cv_bf[:, g * HEAD_DIM:(g + 1) * HEAD_DIM]))
                o = (o / den).astype(jnp.bfloat16)
                mix_scr[js, (2 * g) * HEAD_DIM:(2 * g + 1) * HEAD_DIM] = o[:ATTN_BLOCK]
                mix_scr[js, (2 * g + 1) * HEAD_DIM:(2 * g + 2) * HEAD_DIM] = o[ATTN_BLOCK:]
                yield
    else:
        head_row = lax.broadcasted_iota(jnp.int32, (2 * tile, 1), 0) < tile
        for q in range(n_seq):
            qs = slice(q * tile, (q + 1) * tile)
            for g in range(N_KV_HEADS):
                q2 = jnp.concatenate([q_scr[2 * g, qs, :], q_scr[2 * g + 1, qs, :]], axis=0)
                s = _dot_nt(q2, k_scr[g, qs, :])
                sink = jnp.where(head_row, r.sink[layer, 2 * g] * LOG2_E, r.sink[layer, 2 * g + 1] * LOG2_E)
                m = jnp.maximum(jnp.max(s, axis=-1, keepdims=True), sink)
                yield
                p = jnp.exp2(s - m)
                den = jnp.sum(p, axis=-1, keepdims=True) + jnp.exp2(sink - m)
                yield
                o = (_dot(p.astype(jnp.bfloat16), v_scr[g, qs, :]) / den).astype(jnp.bfloat16)
                mix_scr[qs, (2 * g) * HEAD_DIM:(2 * g + 1) * HEAD_DIM] = o[:tile]
                mix_scr[qs, (2 * g + 1) * HEAD_DIM:(2 * g + 2) * HEAD_DIM] = o[tile:]
                yield

    srows = tile + 2 * MARGIN
    lane256 = lax.broadcasted_iota(jnp.int32, (1, POOL_W), 1)
    grp = lane256 // (POOL_W // POOL_GROUPS)
    half_w = jnp.where(grp == 0, 1, jnp.where(grp == 1, 2, jnp.where(grp == 2, 4, 8)))
    tpos = ti * tile + lax.broadcasted_iota(jnp.int32, (tile, 1), 0)
    cnt = (jnp.minimum(tpos + half_w, seq_len) - jnp.maximum(tpos - half_w, 0)).astype(jnp.float32)
    pool_cols = slice(ATTN_W, ATTN_W + POOL_W)
    conv_cols = slice(ATTN_W + POOL_W, ATTN_W + POOL_W + CONV_W)
    first_tap = MARGIN - CONV_HALF

    for q in range(n_seq):
        base = q * rows
        qs = slice(q * tile, (q + 1) * tile)

        def fill_slab(fn):
            slab_scr[MARGIN:MARGIN + tile, :] = fn(base + c0, base + c1)
            zeros = jnp.zeros((MARGIN, 256), jnp.float32)
            if halo:
                left = fn(c0 - MARGIN, c0)
                right = fn(c1, c1 + MARGIN)
                slab_scr[0:MARGIN, :] = jnp.where(is_first, zeros, left)
                slab_scr[MARGIN + tile:srows, :] = jnp.where(is_last_tile, zeros, right)
            else:
                slab_scr[0:MARGIN, :] = zeros
                slab_scr[MARGIN + tile:srows, :] = zeros

        fill_slab(lambda a, b: proj_scr[a:b, C_POOL:C_POOL + POOL_W])
        xs = slab_scr[...]
        sums = []
        acc = xs
        shift_by = 1
        for size in POOL_SIZES:
            acc = acc + pltpu.roll(acc, shift_by, 0)
            shift_by = size
            shift = size // 2 - 1
            win = acc if shift == 0 else pltpu.roll(acc, srows - shift, 0)
            sums.append(win[MARGIN:MARGIN + tile])
        wsum = jnp.where(grp == 0, sums[0], jnp.where(grp == 1, sums[1], jnp.where(grp == 2, sums[2], sums[3])))
        pooled = wsum / cnt - xs[MARGIN:MARGIN + tile]
        mix_scr[qs, pool_cols] = pooled.astype(jnp.bfloat16)
        yield

        def glu(a, b):
            av = proj_scr[a:b, C_CONV:C_CONV + CONV_W]
            gv = proj_scr[a:b, C_CONV + CONV_W:C_CONV + 2 * CONV_W]
            return av * jax.nn.sigmoid(gv)

        fill_slab(glu)
        for blk in range(tile // CONV_ROWS):
            b0 = blk * CONV_ROWS
            y = None
            for rr in range(8):
                z = None
                for a in range((first_tap + CONV_WIDTH + 7) // 8):
                    o = 8 * a + rr
                    if first_tap <= o < first_tap + CONV_WIDTH:
                        j = o - first_tap
                        term = slab_scr[b0 + 8 * a:b0 + 8 * a + CONV_ROWS + 8, :] * r.conv_dw[0, j:j + 1, :]
                        z = term if z is None else z + term
                zr = z[rr:rr + CONV_ROWS]
                y = zr if y is None else y + zr
            y = _rms(y + r.conv_b[0], r.conv_norm[0])
            y = y * jax.nn.sigmoid(y)
            mix_scr[q * tile + b0:q * tile + b0 + CONV_ROWS, conv_cols] = y.astype(jnp.bfloat16)
            yield

    pool_out = _dot(mix_scr[:, pool_cols], r.pool_w[0]) * r.pool_scale[0]
    mix_scr[:, pool_cols] = pool_out.astype(jnp.bfloat16)
    conv_out = _dot(mix_scr[:, conv_cols], r.conv_pw[0])
    mix_scr[:, conv_cols] = conv_out.astype(jnp.bfloat16)
    yield

    gm_b = r.gm_b[0]
    for q in range(n_seq):
        pr = slice(q * rows + c0, q * rows + c1)
        gu = jax.nn.gelu(proj_scr[pr, C_GM:C_GM + GM_W])
        gv = jax.nn.gelu(proj_scr[pr, C_GM + GM_W:C_GM + 2 * GM_W])
        gv = _rms(gv, r.gm_norm[0]).astype(jnp.bfloat16)
        for c in range(tile // GM_CHUNK):
            cs = slice(c * GM_CHUNK, (c + 1) * GM_CHUNK)
            parts = []
            for g in range(GM_GROUPS):
                parts.append(_dot(r.gm_ws[0, g], gv[cs, g * GM_GROUP_W:(g + 1) * GM_GROUP_W]))
            sv = jnp.concatenate(parts, axis=-1) + gm_b
            ms = slice(q * tile + c * GM_CHUNK, q * tile + (c + 1) * GM_CHUNK)
            mix_scr[ms, ATTN_W + POOL_W + CONV_W:D_MODEL] = (gu[cs] * sv).astype(jnp.bfloat16)
        yield

    mixed = _dot(mix_scr[...], r.w_out[0])
    for q in range(n_seq):
        qs = slice(q * tile, (q + 1) * tile)
        x1 = r.x[q] + g1 * mixed[qs]
        r.x1_scr[slot, qs, :] = x1
        r.h2_scr[slot, qs, :] = (_rms(x1, norm2) * (1.0 + sc2) + sh2).astype(jnp.bfloat16)


def _mixer_pieces(cfg):
    attn = (cfg.tile // ATTN_BLOCK if cfg.latent else cfg.n_seq) * N_KV_HEADS
    return 1 + 3 * attn +cfg.n_seq * (1 + cfg.tile // CONV_ROWS) + 1 + cfg.n_seq


def _layer_kernel(*refs, cfg):
    names = []
    if cfg.halo:
        names += ['xprev', 'x', 'xnext']
    else:
        names += ['x']
    names += ['mod']
    if cfg.latent:
        names += ['modp']
    names += ['norm1', 'norm2', 'w_in', 'w_out', 'sink', 'pool_w', 'pool_scale', 'conv_dw', 'conv_b', 'conv_norm',
              'conv_pw', 'gm_norm', 'gm_ws', 'gm_b', 'w1', 'w2']
    if cfg.latent:
        names += ['cos', 'sin', 'ck', 'cv']
    if cfg.last:
        names += ['fnorm']
    names += ['out']
    if not cfg.latent:
        names += ['kout', 'vout']
    names += ['h_scr', 'proj_scr', 'q_scr', 'k_scr', 'v_scr', 'slab_scr', 'mix_scr', 'acc_scr', 'x1_scr', 'h2_scr']
    assert len(names) == len(refs)
    r = types.SimpleNamespace(**dict(zip(names, refs)))
    if not cfg.latent:
        r.modp = r.mod

    crows = cfg.n_seq * cfg.tile

    @pl.when(pl.program_id(0) == 0)
    def _():
        r.x1_scr[1] = jnp.zeros((crows, D_MODEL), jnp.float32)
        r.h2_scr[1] = jnp.zeros((crows, D_MODEL), jnp.bfloat16)

    mlp = _mlp_half(r, cfg)
    mixer = _mixer_half(r, cfg)
    n_mlp = 2 * (D_FF // FF_CHUNK)
    n_mix = _mixer_pieces(cfg)
    tail_mlp = 4
    next(mlp)
    next(mixer)
    done = 1
    for i in range(n_mix):
        next(mixer)
        target = ((i + 1) * (n_mlp - tail_mlp)) // n_mix
        while done < target:
            next(mlp)
            done += 1
    for gen in (mixer, mlp):
        for _ in gen:
            pass


def _const_spec(shape, index):
    return pl.BlockSpec(shape, lambda s: index, pipeline_mode=pl.Buffered(1))


def _layer_call(x, layer, mod_all, params, *, tile, latent, last, rope=None, cache=None, final_norm=None):
    batch, seq_len, _ = x.shape
    halo = HALO if latent else 0
    rows = tile + 2 * halo
    n_tiles = seq_len // tile
    n_seq = STEP_ROWS // tile
    assert seq_len % tile == 0 and tile % ATTN_BLOCK == 0 and STEP_ROWS % tile == 0
    if latent:
        assert n_seq == 1
    else:
        assert n_tiles == 1 and batch % n_seq == 0
    n_steps = (batch // n_seq) * n_tiles
    last_step = n_steps - 1

    def cur(s):
        return jnp.minimum(s, last_step)

    def prev(s):
        return jnp.maximum(s - 1, 0)

    in_specs = []
    args = []
    if halo:
        hb = tile // halo
        n_hblk = seq_len // halo
        in_specs.append(pl.BlockSpec(
            (1, halo, D_MODEL),
            lambda s: (cur(s) // n_tiles, jnp.maximum((cur(s) % n_tiles) * hb - 1, 0), 0)))
        args.append(x)
    in_specs.append(pl.BlockSpec((n_seq, tile, D_MODEL), lambda s: (cur(s) // n_tiles, cur(s) % n_tiles, 0)))
    args.append(x)
    if halo:
        in_specs.append(pl.BlockSpec(
            (1, halo, D_MODEL),
            lambda s: (cur(s) // n_tiles, jnp.minimum((cur(s) % n_tiles + 1) * hb, n_hblk - 1), 0)))
        args.append(x)

    if latent:
        in_specs.append(pl.BlockSpec((1, 1, N_MOD * D_MODEL),
                                     lambda s: (layer * MOD_ROWS + cur(s) // n_tiles, 0, 0)))
        args.append(mod_all)
        in_specs.append(pl.BlockSpec((1, 1, N_MOD * D_MODEL),
                                     lambda s: (layer * MOD_ROWS + prev(s) // n_tiles, 0, 0)))
        args.append(mod_all)
    else:
        in_specs.append(pl.BlockSpec((1, 1, N_MOD * D_MODEL), lambda s: (layer * MOD_ROWS + CTX_MOD_ROW, 0, 0)))
        args.append(mod_all)

    def add_const(arr, block):
        idx = (layer,) + (0,) * (len(block) - 1)
        in_specs.append(_const_spec(block, idx))
        args.append(arr)

    add_const(params['norm1'], (1, 1, D_MODEL))
    add_const(params['norm2'], (1, 1, D_MODEL))
    add_const(params['w_in'], (1, D_MODEL, IN_W))
    add_const(params['w_out'], (1, D_MODEL, D_MODEL))
    in_specs.append(pl.BlockSpec(memory_space=pltpu.SMEM))
    args.append(params['attn_sink'])
    add_const(params['pool_w'], (1, POOL_W, POOL_W))
    add_const(params['pool_scale'], (1, 1, POOL_W))
    add_const(params['conv_dw'], (1, CONV_WIDTH, CONV_W))
    add_const(params['conv_b'], (1, 1, CONV_W))
    add_const(params['conv_norm'], (1, 1, CONV_W))
    add_const(params['conv_pw'], (1, CONV_W, CONV_W))
    add_const(params['gm_norm'], (1, 1, GM_W))
    add_const(params['gm_ws'], (1, GM_GROUPS, GM_CHUNK, GM_CHUNK))
    add_const(params['gm_b'], (1, GM_CHUNK, GM_W))
    add_const(params['w_mlp1'], (1, D_MODEL, D_FF))
    add_const(params['w_mlp2'], (1, D_FF, D_MODEL))
    if latent:
        cos_t, sin_t = rope
        in_specs.append(_const_spec(cos_t.shape, (0, 0)))
        args.append(cos_t)
        in_specs.append(_const_spec(sin_t.shape, (0, 0)))
        args.append(sin_t)
        ck, cv = cache
        past = ck.shape[2]
        in_specs.append(pl.BlockSpec((1, 1, past, KV_W), lambda s: (cur(s) // n_tiles, layer, 0, 0)))
        args.append(ck)
        in_specs.append(pl.BlockSpec((1, 1, past, KV_W), lambda s: (cur(s) // n_tiles, layer, 0, 0)))
        args.append(cv)
    if last:
        in_specs.append(_const_spec((1, D_MODEL), (0, 0)))
        args.append(final_norm)

    out_shape = [jax.ShapeDtypeStruct(x.shape, jnp.float32)]
    out_specs = [pl.BlockSpec((n_seq, tile, D_MODEL), lambda s: (prev(s) // n_tiles, prev(s) % n_tiles, 0))]
    if not latent:
        for _ in range(2):
            out_shape.append(jax.ShapeDtypeStruct((batch, seq_len, KV_W), jnp.float32))
            out_specs.append(pl.BlockSpec((n_seq, tile, KV_W), lambda s: (cur(s), 0, 0)))

    crows = n_seq * tile
    scratch = [
        pltpu.VMEM((n_seq * rows, D_MODEL), jnp.bfloat16),
        pltpu.VMEM((n_seq * rows, IN_W), jnp.float32),
        pltpu.VMEM((N_Q_HEADS, crows, HEAD_DIM), jnp.bfloat16),
        pltpu.VMEM((N_KV_HEADS, n_seq * rows, HEAD_DIM), jnp.bfloat16),
        pltpu.VMEM((N_KV_HEADS, n_seq * rows, HEAD_DIM), jnp.bfloat16),
        pltpu.VMEM((tile + 2 * MARGIN, 256), jnp.float32),
        pltpu.VMEM((crows, D_MODEL), jnp.bfloat16),
        pltpu.VMEM((crows, D_MODEL), jnp.float32),
        pltpu.VMEM((2, crows, D_MODEL), jnp.float32),
        pltpu.VMEM((2, crows, D_MODEL), jnp.bfloat16),
    ]
    cfg = types.SimpleNamespace(layer=layer, tile=tile, n_seq=n_seq, halo=halo, seq_len=seq_len, n_steps=n_steps,
                                latent=latent, last=last)
    kern = functools.partial(_layer_kernel, cfg=cfg)
    outs = pl.pallas_call(
        kern,
        grid=(n_steps + 1,),
        in_specs=in_specs,
        out_specs=out_specs,
        out_shape=out_shape,
        scratch_shapes=scratch,
        compiler_params=pltpu.CompilerParams(
            dimension_semantics=("arbitrary",),
            vmem_limit_bytes=VMEM_LIMIT_BYTES),
        name=("latent" if latent else "context") + "_layer",
    )(*args)
    return outs


def _rope_tables(seq_len):
    rows = seq_len // GRID_W
    row = jnp.repeat(jnp.arange(rows), GRID_W).astype(jnp.float32)
    col = jnp.tile(jnp.arange(GRID_W), rows).astype(jnp.float32)
    inv = ROPE_BASE ** (-jnp.arange(ROPE_PAIRS, dtype=jnp.float32) / ROPE_PAIRS)
    ang = jnp.concatenate([row[:, None] * inv, col[:, None] * inv], axis=-1)
    cos, sin = jnp.cos(ang), jnp.sin(ang)
    cos_t = jnp.tile(jnp.concatenate([cos, cos], axis=-1), (1, 2))
    sin_t = jnp.tile(jnp.concatenate([-sin, sin], axis=-1), (1, 2))
    pad = ((HALO, HALO), (0, 0))
    return jnp.pad(cos_t, pad), jnp.pad(sin_t, pad)


def _prep_params(norm1, norm2, w_in, w_out, attn_sink, pool_w, pool_scale, conv_dw, conv_b, conv_norm,
                 conv_pw, gm_norm, gm_ws, gm_b, w_mlp1, w_mlp2):
    depth = w_in.shape[0]
    bf = jnp.bfloat16
    gw = POOL_W // POOL_GROUPS
    pool_bd = jnp.zeros((depth, POOL_W, POOL_W), jnp.float32)
    for g in range(POOL_GROUPS):
        pool_bd = pool_bd.at[:, g * gw:(g + 1) * gw, g * gw:(g + 1) * gw].set(pool_w[:, g])
    gm_b_full = jnp.repeat(jnp.transpose(gm_b, (0, 2, 1)), GM_GROUP_W, axis=-1)
    return {
        'norm1': norm1.reshape(depth, 1, D_MODEL),
        'norm2': norm2.reshape(depth, 1, D_MODEL),
        'w_in': w_in.astype(bf),
        'w_out': w_out.astype(bf),
        'attn_sink': attn_sink,
        'pool_w': pool_bd.astype(bf),
        'pool_scale': pool_scale.reshape(depth, 1, POOL_W),
        'conv_dw': conv_dw,
        'conv_b': conv_b.reshape(depth, 1, CONV_W),
        'conv_norm': conv_norm.reshape(depth, 1, CONV_W),
        'conv_pw': conv_pw.astype(bf),
        'gm_norm': gm_norm.reshape(depth, 1, GM_W),
        'gm_ws': gm_ws.astype(bf),
        'gm_b': gm_b_full,
        'w_mlp1': w_mlp1.astype(bf),
        'w_mlp2': w_mlp2.astype(bf),
    }


def kernel(x_prompt, x_sample, cache_k, cache_v, c, c_ctx, w_ada, b_ada, norm1, norm2, w_in, w_out, attn_sink,
           pool_w, pool_scale, conv_dw, conv_b, conv_norm, conv_pw, gm_norm, gm_ws, gm_b, w_mlp1, w_mlp2,
           final_norm):
    depth = w_in.shape[0]
    dec_batch, dec_seq, _ = x_sample.shape
    batch, seq, _ = x_prompt.shape
    past = cache_k.shape[2]

    cond = jnp.zeros((MOD_ROWS, D_MODEL), jnp.float32)
    cond = cond.at[:dec_batch].set(c).at[CTX_MOD_ROW].set(c_ctx)
    mod_all = _ada_mod(cond, w_ada, b_ada).reshape(depth * MOD_ROWS, 1, N_MOD * D_MODEL)

    params = _prep_params(norm1, norm2, w_in, w_out, attn_sink, pool_w, pool_scale, conv_dw, conv_b,
                          conv_norm, conv_pw, gm_norm, gm_ws, gm_b, w_mlp1, w_mlp2)
    rope = _rope_tables(dec_seq)
    ck = cache_k.reshape(dec_batch, depth, past, KV_W)
    cv = cache_v.reshape(dec_batch, depth, past, KV_W)
    fnorm = final_norm.reshape(1, D_MODEL)

    xc, xs = x_prompt, x_sample
    ks_out, vs_out = [], []
    for l in range(depth):
        last = l == depth - 1
        xc, k_l, v_l = _layer_call(xc, l, mod_all, params, tile=seq, latent=False, last=last,
                                   final_norm=fnorm if last else None)
        ks_out.append(k_l)
        vs_out.append(v_l)
        (xs,) = _layer_call(xs, l, mod_all, params, tile=STEP_ROWS, latent=True, last=last,
                            rope=rope, cache=(ck, cv), final_norm=fnorm if last else None)
    new_k = jnp.stack(ks_out, axis=1).reshape(batch, depth, seq, N_KV_HEADS, HEAD_DIM)
    new_v = jnp.stack(vs_out, axis=1).reshape(batch, depth, seq, N_KV_HEADS, HEAD_DIM)
    return (xc, xs, new_k, new_v)
```

```python
import functools
import types

import jax
import jax.numpy as jnp
import numpy as np
from jax import lax
from jax.experimental import pallas as pl
from jax.experimental.pallas import tpu as pltpu

D_MODEL = 1024
DEPTH = 4
ATTN_W = 256
POOL_W = 256
CONV_W = 256
GM_W = 256
HEAD_DIM = 64
N_Q_HEADS = 4
N_KV_HEADS = 2
Q_PER_KV = 2
KV_W = 128
ATTN_BLOCK = 128
GRID_W = 64
ROPE_BASE = 10000.0
ROPE_PAIRS = 16
POOL_GROUPS = 4
POOL_SIZES = (2, 4, 8, 16)
CONV_WIDTH = 31
CONV_HALF = 15
GM_GROUPS = 4
GM_GROUP_W = 64
GM_CHUNK = 128
D_FF = 4096
N_MOD = 6
EPS = 1e-6
NEG_INF = -1e30
LOG2_E = 1.4426950408889634
IN_W = 1792
C_Q, C_K, C_V, C_POOL, C_CONV, C_GM = 0, 256, 384, 512, 768, 1280

HALO = 128
MARGIN = 16
POOL_EDGE = 16
MOD_ROWS = 16
CTX_MOD_ROW = 8
FF_CHUNK = 512
CONV_ROWS = 64
TAIL_ROWS = 128
STEP_ROWS = 512
VMEM_LIMIT_BYTES = 58 * 1024 * 1024


def _rms(x, g):
    ms = jnp.mean(x * x, axis=-1, keepdims=True)
    return x * lax.rsqrt(ms + EPS) * g


def _dot(a, b):
    return jnp.dot(a, b, preferred_element_type=jnp.float32)


def _dot_nt(a, b):
    return lax.dot_general(a, b, (((1,), (1,)), ((), ())), preferred_element_type=jnp.float32)


def _ada_kernel(cond_ref, w_ref, b_ref, o_ref):
    s = cond_ref[...]
    s = s * jax.nn.sigmoid(s)
    o_ref[0] = _dot(s.astype(jnp.bfloat16), w_ref[0].astype(jnp.bfloat16)) + b_ref[0]


def _ada_mod(cond, w_ada, b_ada):
    tn = 1536
    nt = (N_MOD * D_MODEL) // tn
    return pl.pallas_call(
        _ada_kernel,
        grid=(DEPTH, nt),
        in_specs=[
            pl.BlockSpec((MOD_ROWS, D_MODEL), lambda l, j: (0, 0)),
            pl.BlockSpec((1, D_MODEL, tn), lambda l, j: (l, 0, j)),
            pl.BlockSpec((1, 1, tn), lambda l, j: (l, 0, j)),
        ],
        out_specs=pl.BlockSpec((1, MOD_ROWS, tn), lambda l, j: (l, 0, j)),
        out_shape=jax.ShapeDtypeStruct((DEPTH, MOD_ROWS, N_MOD * D_MODEL), jnp.float32),
        compiler_params=pltpu.CompilerParams(
            dimension_semantics=("arbitrary", "arbitrary"),
            vmem_limit_bytes=VMEM_LIMIT_BYTES),
        name="ada_mod",
    )(cond, w_ada, b_ada.reshape(DEPTH, 1, N_MOD * D_MODEL))


def _mlp_half(r, cfg):
    tile, n_seq = cfg.tile, cfg.n_seq
    slot = 1 - pl.program_id(0) % 2
    g2 = r.modp[0][:, 5 * D_MODEL:6 * D_MODEL]
    n_chunks = D_FF // FF_CHUNK
    for c in range(n_chunks):
        a = _dot(r.h2_scr[slot], r.w1[0, :, c * FF_CHUNK:(c + 1) * FF_CHUNK])
        a = jnp.square(jnp.maximum(a, 0.0)).astype(jnp.bfloat16)
        yield
        w2_c = r.w2[0, c * FF_CHUNK:(c + 1) * FF_CHUNK, :]
        if c < n_chunks - 1:
            f = _dot(a, w2_c)
            if c == 0:
                r.acc_scr[...] = f
            else:
                r.acc_scr[...] += f
            yield
    half = n_seq * tile // 2
    for hq in range(2):
        hs = slice(hq * half, (hq + 1) * half)
        x2 = r.x1_scr[slot, hs, :] + g2 * (r.acc_scr[hs, :] + _dot(a[hs], w2_c))
        if cfg.last:
            x2 = _rms(x2, r.fnorm[...])
        q, t0 = (hq * half) // tile, (hq * half) % tile
        r.out[q, t0:t0 + half, :] = x2
        if hq == 0:
            yield


def _mixer_half(r, cfg):
    tile, n_seq, halo, seq_len, latent, layer = cfg.tile, cfg.n_seq, cfg.halo, cfg.seq_len, cfg.latent, cfg.layer
    rows = tile + 2 * halo
    step = pl.program_id(0)
    slot = step % 2
    cur = jnp.minimum(step, cfg.n_steps - 1)
    n_tiles = seq_len // tile
    ti = cur % n_tiles
    is_first = ti == 0
    is_last_tile = ti == n_tiles - 1
    h_scr, proj_scr, q_scr, k_scr, v_scr, slab_scr, mix_scr = (
        r.h_scr, r.proj_scr, r.q_scr, r.k_scr, r.v_scr, r.slab_scr, r.mix_scr)

    mod = r.mod[0]
    sh1 = mod[:, 0 * D_MODEL:1 * D_MODEL]
    sc1 = mod[:, 1 * D_MODEL:2 * D_MODEL]
    g1 = mod[:, 2 * D_MODEL:3 * D_MODEL]
    sh2 = mod[:, 3 * D_MODEL:4 * D_MODEL]
    sc2 = mod[:, 4 * D_MODEL:5 * D_MODEL]
    norm1 = r.norm1[0]
    norm2 = r.norm2[0]

    def norm_mod1(x):
        return (_rms(x, norm1) * (1.0 + sc1) + sh1).astype(jnp.bfloat16)

    c0 = halo
    c1 = halo + tile
    if halo:
        h_scr[0:halo, :] = norm_mod1(r.xprev[0])
        h_scr[c1:rows, :] = norm_mod1(r.xnext[0])
        h_scr[c0:c1, :] = norm_mod1(r.x[0])
        m0, m1 = c0 - MARGIN, c1 + MARGIN
        proj_scr[c0:c1, C_Q:C_K] = _dot(h_scr[c0:c1, :], r.w_in[0, :, C_Q:C_K])
        proj_scr[:, C_K:C_POOL] = _dot(h_scr[...], r.w_in[0, :, C_K:C_POOL])
        proj_scr[m0:m1, C_POOL:C_GM] = _dot(h_scr[m0:m1, :], r.w_in[0, :, C_POOL:C_GM])
        proj_scr[c0:c1, C_GM:IN_W] = _dot(h_scr[c0:c1, :], r.w_in[0, :, C_GM:IN_W])
    else:
        for q in range(n_seq):
            h_scr[q * tile:(q + 1) * tile, :] = norm_mod1(r.x[q])
        proj_scr[...] = _dot(h_scr[...], r.w_in[0])
    yield

    scale = HEAD_DIM ** -0.5 * LOG2_E
    if latent:
        row0 = ti * tile
        cos_t = r.cos[pl.ds(pl.multiple_of(row0, tile), rows), :]
        sin_t = r.sin[pl.ds(pl.multiple_of(row0, tile), rows), :]
        lane = lax.broadcasted_iota(jnp.int32, (1, 128), 1)
        first_half = (lane % HEAD_DIM) < (HEAD_DIM // 2)

        def rope(z, cs, sn):
            swapped = jnp.where(first_half, pltpu.roll(z, 96, 1), pltpu.roll(z, 32, 1))
            return z * cs + swapped * sn

    k_all = proj_scr[:, C_K:C_K + KV_W]
    if not latent:
        for q in range(n_seq):
            r.kout[q, 0] = k_all[q * tile:(q + 1) * tile]
            r.vout[q, 0] = proj_scr[q * tile:(q + 1) * tile, C_V:C_V + KV_W]
    if latent:
        k_all = rope(k_all, cos_t, sin_t)
    k_bf = k_all.astype(jnp.bfloat16)
    v_bf = proj_scr[:, C_V:C_V + KV_W].astype(jnp.bfloat16)
    for g in range(N_KV_HEADS):
        k_scr[g] = k_bf[:, g * HEAD_DIM:(g + 1) * HEAD_DIM]
        v_scr[g] = v_bf[:, g * HEAD_DIM:(g + 1) * HEAD_DIM]
    for q in range(n_seq):
        for half in range(2):
            qh = proj_scr[q * rows + c0:q * rows + c1, C_Q + half * 128:C_Q + (half + 1) * 128]
            if latent:
                qh = rope(qh, cos_t[c0:c1], sin_t[c0:c1])
            qh = (qh * scale).astype(jnp.bfloat16)
            for s in range(2):
                q_scr[2 * half + s, q * tile:(q + 1) * tile, :] = qh[:, s * HEAD_DIM:(s + 1) * HEAD_DIM]
    yield

    if latent:
        ck_bf = r.ck[0, 0].astype(jnp.bfloat16)
        cv_bf = r.cv[0, 0].astype(jnp.bfloat16)
        r_i = lax.broadcasted_iota(jnp.int32, (2 * ATTN_BLOCK, 3 * ATTN_BLOCK), 0) % ATTN_BLOCK
        c_i = lax.broadcasted_iota(jnp.int32, (2 * ATTN_BLOCK, 3 * ATTN_BLOCK), 1)
        band_ok = jnp.abs(r_i - (c_i - ATTN_BLOCK)) <= HALO
        n_qblk = tile // ATTN_BLOCK
        head_row = lax.broadcasted_iota(jnp.int32, (2 * ATTN_BLOCK, 1), 0) < ATTN_BLOCK
        for j in range(n_qblk):
            bias = jnp.where(band_ok, 0.0, NEG_INF)
            if j == 0:
                lo = jnp.where(is_first, ATTN_BLOCK, 0)
                bias = jnp.where(c_i >= lo, bias, NEG_INF)
            if j == n_qblk - 1:
                hi = jnp.where(is_last_tile, 2 * ATTN_BLOCK, 3 * ATTN_BLOCK)
                bias = jnp.where(c_i < hi, bias, NEG_INF)
            js = slice(j * ATTN_BLOCK, (j + 1) * ATTN_BLOCK)
            ks = slice(j * ATTN_BLOCK, (j + 3) * ATTN_BLOCK)
            for g in range(N_KV_HEADS):
                q2 = jnp.concatenate([q_scr[2 * g, js, :], q_scr[2 * g + 1, js, :]], axis=0)
                s_loc = _dot_nt(q2, k_scr[g, ks, :]) + bias
                s_ctx = _dot_nt(q2, ck_bf[:, g * HEAD_DIM:(g + 1) * HEAD_DIM])
                sink = jnp.where(head_row, r.sink[layer, 2 * g] * LOG2_E, r.sink[layer, 2 * g + 1] * LOG2_E)
                m = jnp.maximum(jnp.maximum(jnp.max(s_loc, axis=-1, keepdims=True),
                                            jnp.max(s_ctx, axis=-1, keepdims=True)), sink)
                yield
                p_loc = jnp.exp2(s_loc - m)
                p_ctx = jnp.exp2(s_ctx - m)
                den = (jnp.sum(p_loc, axis=-1, keepdims=True) + jnp.sum(p_ctx, axis=-1, keepdims=True)
                       + jnp.exp2(sink - m))
                yield
                o = (_dot(p_loc.astype(jnp.bfloat16), v_scr[g, ks, :])
                     + _dot(p_ctx.astype(jnp.bfloat16), cv_bf[:, g * HEAD_DIM:(g + 1) * HEAD_DIM]))
                o = (o / den).astype(jnp.bfloat16)
                mix_scr[js, (2 * g) * HEAD_DIM:(2 * g + 1) * HEAD_DIM] = o[:ATTN_BLOCK]
                mix_scr[js, (2 * g + 1) * HEAD_DIM:(2 * g + 2) * HEAD_DIM] = o[ATTN_BLOCK:]
                yield
    else:
        head_row = lax.broadcasted_iota(jnp.int32, (2 * tile, 1), 0) < tile
        for q in range(n_seq):
            qs = slice(q * tile, (q + 1) * tile)
            for g in range(N_KV_HEADS):
                q2 = jnp.concatenate([q_scr[2 * g, qs, :], q_scr[2 * g + 1, qs, :]], axis=0)
                s = _dot_nt(q2, k_scr[g, qs, :])
                sink = jnp.where(head_row, r.sink[layer, 2 * g] * LOG2_E, r.sink[layer, 2 * g + 1] * LOG2_E)
                m = jnp.maximum(jnp.max(s, axis=-1, keepdims=True), sink)
                yield
                p = jnp.exp2(s - m)
                den = jnp.sum(p, axis=-1, keepdims=True) + jnp.exp2(sink - m)
                yield
                o = (_dot(p.astype(jnp.bfloat16), v_scr[g, qs, :]) / den).astype(jnp.bfloat16)
                mix_scr[qs, (2 * g) * HEAD_DIM:(2 * g + 1) * HEAD_DIM] = o[:tile]
                mix_scr[qs, (2 * g + 1) * HEAD_DIM:(2 * g + 2) * HEAD_DIM] = o[tile:]
                yield

    srows = tile + 2 * MARGIN
    lo_grp = lax.broadcasted_iota(jnp.int32, (1, 128), 1) < POOL_W // POOL_GROUPS
    pool_cols = slice(ATTN_W, ATTN_W + POOL_W)
    conv_cols = slice(ATTN_W + POOL_W, ATTN_W + POOL_W + CONV_W)
    first_tap = MARGIN - CONV_HALF

    for q in range(n_seq):
        base = q * rows
        qs = slice(q * tile, (q + 1) * tile)

        def fill_slab(fn):
            slab_scr[MARGIN:MARGIN + tile, :] = fn(base + c0, base + c1)
            zeros = jnp.zeros((MARGIN, 256), jnp.float32)
            if halo:
                left = fn(c0 - MARGIN, c0)
                right = fn(c1, c1 + MARGIN)
                slab_scr[0:MARGIN, :] = jnp.where(is_first, zeros, left)
                slab_scr[MARGIN + tile:srows, :] = jnp.where(is_last_tile, zeros, right)
            else:
                slab_scr[0:MARGIN, :] = zeros
                slab_scr[MARGIN + tile:srows, :] = zeros

        fill_slab(lambda a, b: proj_scr[a:b, C_POOL:C_POOL + POOL_W])
        for hf in range(2):
            xs = slab_scr[:, hf * 128:(hf + 1) * 128]
            size_lo, size_hi = POOL_SIZES[2 * hf], POOL_SIZES[2 * hf + 1]
            wins = {}
            acc = xs
            shift_by = 1
            for size in POOL_SIZES:
                if size > size_hi:
                    break
                acc = acc + pltpu.roll(acc, shift_by, 0)
                shift_by = size
                if size in (size_lo, size_hi):
                    shift = size // 2 - 1
                    win = acc if shift == 0 else pltpu.roll(acc, srows - shift, 0)
                    wins[size] = win[MARGIN:MARGIN + tile]
            wsum = jnp.where(lo_grp, wins[size_lo], wins[size_hi])
            half_w = jnp.where(lo_grp, size_lo // 2, size_hi // 2)
            inv_full = jnp.where(lo_grp, 1.0 / size_lo, 1.0 / size_hi)

            def edge(row0):
                tpos = ti * tile + row0 + lax.broadcasted_iota(jnp.int32, (POOL_EDGE, 1), 0)
                cnt = jnp.minimum(tpos + half_w, seq_len) - jnp.maximum(tpos - half_w, 0)
                return wsum[row0:row0 + POOL_EDGE] / cnt.astype(jnp.float32)

            mean = jnp.concatenate(
                [edge(0), wsum[POOL_EDGE:tile - POOL_EDGE] * inv_full, edge(tile - POOL_EDGE)], axis=0)
            pooled = mean - xs[MARGIN:MARGIN + tile]
            mix_scr[qs, ATTN_W + hf * 128:ATTN_W + (hf + 1) * 128] = pooled.astype(jnp.bfloat16)
        yield

        def glu(a, b):
            av = proj_scr[a:b, C_CONV:C_CONV + CONV_W]
            gv = proj_scr[a:b, C_CONV + CONV_W:C_CONV + 2 * CONV_W]
            return av * jax.nn.sigmoid(gv)

        fill_slab(glu)
        for blk in range(tile // CONV_ROWS):
            b0 = blk * CONV_ROWS
            y = None
            for rr in range(8):
                z = None
                for a in range((first_tap + CONV_WIDTH + 7) // 8):
                    o = 8 * a + rr
                    if first_tap <= o < first_tap + CONV_WIDTH:
                        j = o - first_tap
                        term = slab_scr[b0 + 8 * a:b0 + 8 * a + CONV_ROWS + 8, :] * r.conv_dw[0, j:j + 1, :]
                        z = term if z is None else z + term
                zr = z[rr:rr + CONV_ROWS]
                y = zr if y is None else y + zr
            y = _rms(y + r.conv_b[0], r.conv_norm[0])
            y = y * jax.nn.sigmoid(y)
            mix_scr[q * tile + b0:q * tile + b0 + CONV_ROWS, conv_cols] = y.astype(jnp.bfloat16)
            yield

    pool_out = _dot(mix_scr[:, pool_cols], r.pool_w[0]) * r.pool_scale[0]
    mix_scr[:, pool_cols] = pool_out.astype(jnp.bfloat16)
    conv_out = _dot(mix_scr[:, conv_cols], r.conv_pw[0])
    mix_scr[:, conv_cols] = conv_out.astype(jnp.bfloat16)
    yield

    gm_b = r.gm_b[0]
    for q in range(n_seq):
        pr = slice(q * rows + c0, q * rows + c1)
        gu = jax.nn.gelu(proj_scr[pr, C_GM:C_GM + GM_W])
        gv = jax.nn.gelu(proj_scr[pr, C_GM + GM_W:C_GM + 2 * GM_W])
        gv = _rms(gv, r.gm_norm[0]).astype(jnp.bfloat16)
        for c in range(tile // GM_CHUNK):
            cs = slice(c * GM_CHUNK, (c + 1) * GM_CHUNK)
            parts = []
            for g in range(GM_GROUPS):
                parts.append(_dot(r.gm_ws[0, g], gv[cs, g * GM_GROUP_W:(g + 1) * GM_GROUP_W]))
            sv = jnp.concatenate(parts, axis=-1) + gm_b
            ms = slice(q * tile + c * GM_CHUNK, q * tile + (c + 1) * GM_CHUNK)
            mix_scr[ms, ATTN_W + POOL_W + CONV_W:D_MODEL] = (gu[cs] * sv).astype(jnp.bfloat16)
        yield

    half = n_seq * tile // 2
    for hq in range(2):
        mixed = _dot(mix_scr[hq * half:(hq + 1) * half, :], r.w_out[0])
        for lo in range(0, half, TAIL_ROWS):
            row = hq * half + lo
            q, t0 = row // tile, row % tile
            x1 = r.x[q, t0:t0 + TAIL_ROWS, :] + g1 * mixed[lo:lo + TAIL_ROWS]
            r.x1_scr[slot, row:row + TAIL_ROWS, :] = x1
            r.h2_scr[slot, row:row + TAIL_ROWS, :] = (_rms(x1, norm2) * (1.0 + sc2) + sh2).astype(jnp.bfloat16)
            yield


def _mixer_pieces(cfg):
    attn = (cfg.tile // ATTN_BLOCK if cfg.latent else cfg.n_seq) * N_KV_HEADS
    return 1 + 3 * attn +cfg.n_seq * (1 + cfg.tile // CONV_ROWS) + 1 + cfg.n_seq


def _layer_kernel(*refs, cfg):
    names = []
    if cfg.halo:
        names += ['xprev', 'x', 'xnext']
    else:
        names += ['x']
    names += ['mod']
    if cfg.latent:
        names += ['modp']
    names += ['norm1', 'norm2', 'w_in', 'w_out', 'sink', 'pool_w', 'pool_scale', 'conv_dw', 'conv_b', 'conv_norm',
              'conv_pw', 'gm_norm', 'gm_ws', 'gm_b', 'w1', 'w2']
    if cfg.latent:
        names += ['cos', 'sin', 'ck', 'cv']
    if cfg.last:
        names += ['fnorm']
    if not cfg.latent:
        names += ['kacc', 'vacc']
    names += ['out']
    if not cfg.latent:
        names += ['kout', 'vout']
    names += ['h_scr', 'proj_scr', 'q_scr', 'k_scr', 'v_scr', 'slab_scr', 'mix_scr', 'acc_scr', 'x1_scr', 'h2_scr']
    assert len(names) == len(refs)
    r = types.SimpleNamespace(**dict(zip(names, refs)))
    if not cfg.latent:
        r.modp = r.mod

    crows = cfg.n_seq * cfg.tile

    @pl.when(pl.program_id(0) == 0)
    def _():
        r.x1_scr[1] = jnp.zeros((crows, D_MODEL), jnp.float32)
        r.h2_scr[1] = jnp.zeros((crows, D_MODEL), jnp.bfloat16)

    mlp = _mlp_half(r, cfg)
    mixer = _mixer_half(r, cfg)
    n_mlp = 2 * (D_FF // FF_CHUNK)
    n_mix = _mixer_pieces(cfg)
    tail_mlp = 4
    next(mlp)
    next(mixer)
    done = 1
    for i in range(n_mix):
        next(mixer)
        target = ((i + 1) * (n_mlp - tail_mlp)) // n_mix
        while done < target:
            next(mlp)
            done += 1
    for _ in mixer:
        if done < n_mlp:
            next(mlp)
            done += 1
    for _ in mlp:
        pass


def _const_spec(shape, index):
    return pl.BlockSpec(shape, lambda s: index, pipeline_mode=pl.Buffered(1))


def _layer_call(x, layer, mod_all, params, *, tile, latent, last, rope=None, cache=None, final_norm=None,
                kv_acc=None):
    batch, seq_len, _ = x.shape
    halo = HALO if latent else 0
    rows = tile + 2 * halo
    n_tiles = seq_len // tile
    n_seq = STEP_ROWS // tile
    assert seq_len % tile == 0 and tile % ATTN_BLOCK == 0 and STEP_ROWS % tile == 0
    if latent:
        assert n_seq == 1
    else:
        assert n_tiles == 1 and batch % n_seq == 0
    n_steps = (batch // n_seq) * n_tiles
    last_step = n_steps - 1

    def cur(s):
        return jnp.minimum(s, last_step)

    def prev(s):
        return jnp.maximum(s - 1, 0)

    in_specs = []
    args = []
    if halo:
        hb = tile // halo
        n_hblk = seq_len // halo
        in_specs.append(pl.BlockSpec(
            (1, halo, D_MODEL),
            lambda s: (cur(s) // n_tiles, jnp.maximum((cur(s) % n_tiles) * hb - 1, 0), 0)))
        args.append(x)
    in_specs.append(pl.BlockSpec((n_seq, tile, D_MODEL), lambda s: (cur(s) // n_tiles, cur(s) % n_tiles, 0)))
    args.append(x)
    if halo:
        in_specs.append(pl.BlockSpec(
            (1, halo, D_MODEL),
            lambda s: (cur(s) // n_tiles, jnp.minimum((cur(s) % n_tiles + 1) * hb, n_hblk - 1), 0)))
        args.append(x)

    if latent:
        in_specs.append(pl.BlockSpec((1, 1, N_MOD * D_MODEL),
                                     lambda s: (layer * MOD_ROWS + cur(s) // n_tiles, 0, 0)))
        args.append(mod_all)
        in_specs.append(pl.BlockSpec((1, 1, N_MOD * D_MODEL),
                                     lambda s: (layer * MOD_ROWS + prev(s) // n_tiles, 0, 0)))
        args.append(mod_all)
    else:
        in_specs.append(pl.BlockSpec((1, 1, N_MOD * D_MODEL), lambda s: (layer * MOD_ROWS + CTX_MOD_ROW, 0, 0)))
        args.append(mod_all)

    def add_const(arr, block):
        idx = (layer,) + (0,) * (len(block) - 1)
        in_specs.append(_const_spec(block, idx))
        args.append(arr)

    add_const(params['norm1'], (1, 1, D_MODEL))
    add_const(params['norm2'], (1, 1, D_MODEL))
    add_const(params['w_in'], (1, D_MODEL, IN_W))
    add_const(params['w_out'], (1, D_MODEL, D_MODEL))
    in_specs.append(pl.BlockSpec(memory_space=pltpu.SMEM))
    args.append(params['attn_sink'])
    add_const(params['pool_w'], (1, POOL_W, POOL_W))
    add_const(params['pool_scale'], (1, 1, POOL_W))
    add_const(params['conv_dw'], (1, CONV_WIDTH, CONV_W))
    add_const(params['conv_b'], (1, 1, CONV_W))
    add_const(params['conv_norm'], (1, 1, CONV_W))
    add_const(params['conv_pw'], (1, CONV_W, CONV_W))
    add_const(params['gm_norm'], (1, 1, GM_W))
    add_const(params['gm_ws'], (1, GM_GROUPS, GM_CHUNK, GM_CHUNK))
    add_const(params['gm_b'], (1, GM_CHUNK, GM_W))
    add_const(params['w_mlp1'], (1, D_MODEL, D_FF))
    add_const(params['w_mlp2'], (1, D_FF, D_MODEL))
    if latent:
        cos_t, sin_t = rope
        in_specs.append(_const_spec(cos_t.shape, (0, 0)))
        args.append(cos_t)
        in_specs.append(_const_spec(sin_t.shape, (0, 0)))
        args.append(sin_t)
        ck, cv = cache
        past = ck.shape[2]
        in_specs.append(pl.BlockSpec((1, 1, past, KV_W), lambda s: (cur(s) // n_tiles, layer, 0, 0)))
        args.append(ck)
        in_specs.append(pl.BlockSpec((1, 1, past, KV_W), lambda s: (cur(s) // n_tiles, layer, 0, 0)))
        args.append(cv)
    if last:
        in_specs.append(_const_spec((1, D_MODEL), (0, 0)))
        args.append(final_norm)

    out_shape = [jax.ShapeDtypeStruct(x.shape, jnp.float32)]
    out_specs = [pl.BlockSpec((n_seq, tile, D_MODEL), lambda s: (prev(s) // n_tiles, prev(s) % n_tiles, 0))]
    aliases = {}
    if not latent:
        for i, acc in enumerate(kv_acc):
            in_specs.append(pl.BlockSpec(memory_space=pl.ANY))
            args.append(acc)
            aliases[len(args) - 1] = 1 + i
            out_shape.append(jax.ShapeDtypeStruct(acc.shape, jnp.float32))
            out_specs.append(pl.BlockSpec((n_seq, 1, tile, KV_W), lambda s: (cur(s), layer, 0, 0)))

    crows = n_seq * tile
    scratch = [
        pltpu.VMEM((n_seq * rows, D_MODEL), jnp.bfloat16),
        pltpu.VMEM((n_seq * rows, IN_W), jnp.float32),
        pltpu.VMEM((N_Q_HEADS, crows, HEAD_DIM), jnp.bfloat16),
        pltpu.VMEM((N_KV_HEADS, n_seq * rows, HEAD_DIM), jnp.bfloat16),
        pltpu.VMEM((N_KV_HEADS, n_seq * rows, HEAD_DIM), jnp.bfloat16),
        pltpu.VMEM((tile + 2 * MARGIN, 256), jnp.float32),
        pltpu.VMEM((crows, D_MODEL), jnp.bfloat16),
        pltpu.VMEM((crows, D_MODEL), jnp.float32),
        pltpu.VMEM((2, crows, D_MODEL), jnp.float32),
        pltpu.VMEM((2, crows, D_MODEL), jnp.bfloat16),
    ]
    cfg = types.SimpleNamespace(layer=layer, tile=tile, n_seq=n_seq, halo=halo, seq_len=seq_len, n_steps=n_steps,
                                latent=latent, last=last)
    kern = functools.partial(_layer_kernel, cfg=cfg)
    outs = pl.pallas_call(
        kern,
        grid=(n_steps + 1,),
        in_specs=in_specs,
        out_specs=out_specs,
        out_shape=out_shape,
        input_output_aliases=aliases,
        scratch_shapes=scratch,
        compiler_params=pltpu.CompilerParams(
            dimension_semantics=("arbitrary",),
            vmem_limit_bytes=VMEM_LIMIT_BYTES),
        name=("latent" if latent else "context") + "_layer",
    )(*args)
    return outs


def _rope_tables(seq_len):
    rows = seq_len // GRID_W
    row = np.repeat(np.arange(rows), GRID_W).astype(np.float32)
    col = np.tile(np.arange(GRID_W), rows).astype(np.float32)
    inv = (np.float32(ROPE_BASE) ** (-np.arange(ROPE_PAIRS, dtype=np.float32) / np.float32(ROPE_PAIRS))).astype(
        np.float32)
    ang = np.concatenate([row[:, None] * inv, col[:, None] * inv], axis=-1)
    cos, sin = np.cos(ang.astype(np.float64)), np.sin(ang.astype(np.float64))
    cos_t = np.tile(np.concatenate([cos, cos], axis=-1), (1, 2)).astype(np.float32)
    sin_t = np.tile(np.concatenate([-sin, sin], axis=-1), (1, 2)).astype(np.float32)
    pad = ((HALO, HALO), (0, 0))
    return jnp.asarray(np.pad(cos_t, pad)), jnp.asarray(np.pad(sin_t, pad))


def _prep_params(norm1, norm2, w_in, w_out, attn_sink, pool_w, pool_scale, conv_dw, conv_b, conv_norm,
                 conv_pw, gm_norm, gm_ws, gm_b, w_mlp1, w_mlp2):
    depth = w_in.shape[0]
    bf = jnp.bfloat16
    gw = POOL_W // POOL_GROUPS
    pool_bd = jnp.zeros((depth, POOL_W, POOL_W), jnp.float32)
    for g in range(POOL_GROUPS):
        pool_bd = pool_bd.at[:, g * gw:(g + 1) * gw, g * gw:(g + 1) * gw].set(pool_w[:, g])
    gm_b_full = jnp.repeat(jnp.transpose(gm_b, (0, 2, 1)), GM_GROUP_W, axis=-1)
    return {
        'norm1': norm1.reshape(depth, 1, D_MODEL),
        'norm2': norm2.reshape(depth, 1, D_MODEL),
        'w_in': w_in.astype(bf),
        'w_out': w_out.astype(bf),
        'attn_sink': attn_sink,
        'pool_w': pool_bd.astype(bf),
        'pool_scale': pool_scale.reshape(depth, 1, POOL_W),
        'conv_dw': conv_dw,
        'conv_b': conv_b.reshape(depth, 1, CONV_W),
        'conv_norm': conv_norm.reshape(depth, 1, CONV_W),
        'conv_pw': conv_pw.astype(bf),
        'gm_norm': gm_norm.reshape(depth, 1, GM_W),
        'gm_ws': gm_ws.astype(bf),
        'gm_b': gm_b_full,
        'w_mlp1': w_mlp1.astype(bf),
        'w_mlp2': w_mlp2.astype(bf),
    }


def kernel(x_prompt, x_sample, cache_k, cache_v, c, c_ctx, w_ada, b_ada, norm1, norm2, w_in, w_out, attn_sink,
           pool_w, pool_scale, conv_dw, conv_b, conv_norm, conv_pw, gm_norm, gm_ws, gm_b, w_mlp1, w_mlp2,
           final_norm):
    depth = w_in.shape[0]
    dec_batch, dec_seq, _ = x_sample.shape
    batch, seq, _ = x_prompt.shape
    past = cache_k.shape[2]

    cond = jnp.zeros((MOD_ROWS, D_MODEL), jnp.float32)
    cond = cond.at[:dec_batch].set(c).at[CTX_MOD_ROW].set(c_ctx)
    mod_all = _ada_mod(cond, w_ada, b_ada).reshape(depth * MOD_ROWS, 1, N_MOD * D_MODEL)

    params = _prep_params(norm1, norm2, w_in, w_out, attn_sink, pool_w, pool_scale, conv_dw, conv_b,
                          conv_norm, conv_pw, gm_norm, gm_ws, gm_b, w_mlp1, w_mlp2)
    rope = _rope_tables(dec_seq)
    ck = cache_k.reshape(dec_batch, depth, past, KV_W)
    cv = cache_v.reshape(dec_batch, depth, past, KV_W)
    fnorm = final_norm.reshape(1, D_MODEL)

    xc, xs = x_prompt, x_sample
    new_k = jnp.zeros((batch, depth, seq, KV_W), jnp.float32)
    new_v = jnp.zeros((batch, depth, seq, KV_W), jnp.float32)
    for l in range(depth):
        last = l == depth - 1
        xc, new_k, new_v = _layer_call(xc, l, mod_all, params, tile=seq, latent=False, last=last,
                                       final_norm=fnorm if last else None, kv_acc=(new_k, new_v))
        (xs,) = _layer_call(xs, l, mod_all, params, tile=STEP_ROWS, latent=True, last=last,
                            rope=rope, cache=(ck, cv), final_norm=fnorm if last else None)
    new_k = new_k.reshape(batch, depth, seq, N_KV_HEADS, HEAD_DIM)
    new_v = new_v.reshape(batch, depth, seq, N_KV_HEADS, HEAD_DIM)
    return (xc, xs, new_k, new_v)
```

```python
import functools
import types

import jax
import jax.numpy as jnp
import numpy as np
from jax import lax
from jax.experimental import pallas as pl
from jax.experimental.pallas import tpu as pltpu

D_MODEL = 1024
DEPTH = 4
ATTN_W = 256
POOL_W = 256
CONV_W = 256
GM_W = 256
HEAD_DIM = 64
N_Q_HEADS = 4
N_KV_HEADS = 2
Q_PER_KV = 2
KV_W = 128
ATTN_BLOCK = 128
GRID_W = 64
ROPE_BASE = 10000.0
ROPE_PAIRS = 16
POOL_GROUPS = 4
POOL_SIZES = (2, 4, 8, 16)
CONV_WIDTH = 31
CONV_HALF = 15
GM_GROUPS = 4
GM_GROUP_W = 64
GM_CHUNK = 128
D_FF = 4096
N_MOD = 6
EPS = 1e-6
NEG_INF = -1e30
LOG2_E = 1.4426950408889634
IN_W = 1792
C_Q, C_K, C_V, C_POOL, C_CONV, C_GM = 0, 256, 384, 512, 768, 1280

HALO = 128
MARGIN = 16
POOL_EDGE = 16
MOD_ROWS = 16
CTX_MOD_ROW = 8
FF_CHUNK = 512
CONV_ROWS = 64
TAIL_ROWS = 128
STEP_ROWS = 512
VMEM_LIMIT_BYTES = 58 * 1024 * 1024


def _rms(x, g):
    ms = jnp.mean(x * x, axis=-1, keepdims=True)
    return x * lax.rsqrt(ms + EPS) * g


def _dot(a, b):
    return jnp.dot(a, b, preferred_element_type=jnp.float32)


def _dot_nt(a, b):
    return lax.dot_general(a, b, (((1,), (1,)), ((), ())), preferred_element_type=jnp.float32)


def _ada_kernel(cond_ref, w_ref, b_ref, o_ref):
    s = cond_ref[...]
    s = s * jax.nn.sigmoid(s)
    o_ref[0] = _dot(s.astype(jnp.bfloat16), w_ref[0].astype(jnp.bfloat16)) + b_ref[0]


def _ada_mod(cond, w_ada, b_ada):
    tn = 1536
    nt = (N_MOD * D_MODEL) // tn
    return pl.pallas_call(
        _ada_kernel,
        grid=(DEPTH, nt),
        in_specs=[
            pl.BlockSpec((MOD_ROWS, D_MODEL), lambda l, j: (0, 0)),
            pl.BlockSpec((1, D_MODEL, tn), lambda l, j: (l, 0, j)),
            pl.BlockSpec((1, 1, tn), lambda l, j: (l, 0, j)),
        ],
        out_specs=pl.BlockSpec((1, MOD_ROWS, tn), lambda l, j: (l, 0, j)),
        out_shape=jax.ShapeDtypeStruct((DEPTH, MOD_ROWS, N_MOD * D_MODEL), jnp.float32),
        compiler_params=pltpu.CompilerParams(
            dimension_semantics=("arbitrary", "arbitrary"),
            vmem_limit_bytes=VMEM_LIMIT_BYTES),
        name="ada_mod",
    )(cond, w_ada, b_ada.reshape(DEPTH, 1, N_MOD * D_MODEL))


def _mlp_half(r, cfg):
    tile, n_seq = cfg.tile, cfg.n_seq
    slot = 1 - pl.program_id(0) % 2
    g2 = r.modp[0][:, 5 * D_MODEL:6 * D_MODEL]
    n_chunks = D_FF // FF_CHUNK
    for c in range(n_chunks):
        a = _dot(r.h2_scr[slot], r.w1[0, :, c * FF_CHUNK:(c + 1) * FF_CHUNK])
        a = jnp.square(jnp.maximum(a, 0.0)).astype(jnp.bfloat16)
        yield
        w2_c = r.w2[0, c * FF_CHUNK:(c + 1) * FF_CHUNK, :]
        if c < n_chunks - 1:
            f = _dot(a, w2_c)
            if c == 0:
                r.acc_scr[...] = f
            else:
                r.acc_scr[...] += f
            yield
    half = n_seq * tile // 2
    for hq in range(2):
        hs = slice(hq * half, (hq + 1) * half)
        x2 = r.x1_scr[slot, hs, :] + g2 * (r.acc_scr[hs, :] + _dot(a[hs], w2_c))
        if cfg.last:
            x2 = _rms(x2, r.fnorm[...])
        q, t0 = (hq * half) // tile, (hq * half) % tile
        r.out[q, t0:t0 + half, :] = x2
        if hq == 0:
            yield


def _mixer_half(r, cfg):
    tile, n_seq, halo, seq_len, latent, layer = cfg.tile, cfg.n_seq, cfg.halo, cfg.seq_len, cfg.latent, cfg.layer
    rows = tile + 2 * halo
    step = pl.program_id(0)
    slot = step % 2
    cur = jnp.minimum(step, cfg.n_steps - 1)
    n_tiles = seq_len // tile
    ti = cur % n_tiles
    is_first = ti == 0
    is_last_tile = ti == n_tiles - 1
    h_scr, proj_scr, q_scr, k_scr, v_scr, slab_scr, mix_scr = (
        r.h_scr, r.proj_scr, r.q_scr, r.k_scr, r.v_scr, r.slab_scr, r.mix_scr)

    mod = r.mod[0]
    sh1 = mod[:, 0 * D_MODEL:1 * D_MODEL]
    sc1 = mod[:, 1 * D_MODEL:2 * D_MODEL]
    g1 = mod[:, 2 * D_MODEL:3 * D_MODEL]
    sh2 = mod[:, 3 * D_MODEL:4 * D_MODEL]
    sc2 = mod[:, 4 * D_MODEL:5 * D_MODEL]
    norm1 = r.norm1[0]
    norm2 = r.norm2[0]

    def norm_mod1(x):
        return (_rms(x, norm1) * (1.0 + sc1) + sh1).astype(jnp.bfloat16)

    c0 = halo
    c1 = halo + tile
    if halo:
        h_scr[0:halo, :] = norm_mod1(r.xprev[0])
        h_scr[c1:rows, :] = norm_mod1(r.xnext[0])
        h_scr[c0:c1, :] = norm_mod1(r.x[0])
        m0, m1 = c0 - MARGIN, c1 + MARGIN
        proj_scr[c0:c1, C_Q:C_K] = _dot(h_scr[c0:c1, :], r.w_in[0, :, C_Q:C_K])
        proj_scr[:, C_K:C_POOL] = _dot(h_scr[...], r.w_in[0, :, C_K:C_POOL])
        proj_scr[m0:m1, C_POOL:C_GM] = _dot(h_scr[m0:m1, :], r.w_in[0, :, C_POOL:C_GM])
        proj_scr[c0:c1, C_GM:IN_W] = _dot(h_scr[c0:c1, :], r.w_in[0, :, C_GM:IN_W])
    else:
        for q in range(n_seq):
            h_scr[q * tile:(q + 1) * tile, :] = norm_mod1(r.x[q])
        proj_scr[...] = _dot(h_scr[...], r.w_in[0])
    yield

    scale = HEAD_DIM ** -0.5 * LOG2_E
    if latent:
        row0 = ti * tile
        cos_t = r.cos[pl.ds(pl.multiple_of(row0, tile), rows), :]
        sin_t = r.sin[pl.ds(pl.multiple_of(row0, tile), rows), :]
        lane = lax.broadcasted_iota(jnp.int32, (1, 128), 1)
        first_half = (lane % HEAD_DIM) < (HEAD_DIM // 2)

        def rope(z, cs, sn):
            swapped = jnp.where(first_half, pltpu.roll(z, 96, 1), pltpu.roll(z, 32, 1))
            return z * cs + swapped * sn

    k_all = proj_scr[:, C_K:C_K + KV_W]
    if not latent:
        for q in range(n_seq):
            r.kout[q, 0] = k_all[q * tile:(q + 1) * tile]
            r.vout[q, 0] = proj_scr[q * tile:(q + 1) * tile, C_V:C_V + KV_W]
    if latent:
        k_all = rope(k_all, cos_t, sin_t)
    k_bf = k_all.astype(jnp.bfloat16)
    v_bf = proj_scr[:, C_V:C_V + KV_W].astype(jnp.bfloat16)
    for g in range(N_KV_HEADS):
        k_scr[g] = k_bf[:, g * HEAD_DIM:(g + 1) * HEAD_DIM]
        v_scr[g] = v_bf[:, g * HEAD_DIM:(g + 1) * HEAD_DIM]
    for q in range(n_seq):
        for half in range(2):
            qh = proj_scr[q * rows + c0:q * rows + c1, C_Q + half * 128:C_Q + (half + 1) * 128]
            if latent:
                qh = rope(qh, cos_t[c0:c1], sin_t[c0:c1])
            qh = (qh * scale).astype(jnp.bfloat16)
            for s in range(2):
                q_scr[2 * half + s, q * tile:(q + 1) * tile, :] = qh[:, s * HEAD_DIM:(s + 1) * HEAD_DIM]
    yield

    if latent:
        ck_bf = r.ck[0, 0].astype(jnp.bfloat16)
        cv_bf = r.cv[0, 0].astype(jnp.bfloat16)
        r_i = lax.broadcasted_iota(jnp.int32, (2 * ATTN_BLOCK, 3 * ATTN_BLOCK), 0) % ATTN_BLOCK
        c_i = lax.broadcasted_iota(jnp.int32, (2 * ATTN_BLOCK, 3 * ATTN_BLOCK), 1)
        band_ok = jnp.abs(r_i - (c_i - ATTN_BLOCK)) <= HALO
        n_qblk = tile // ATTN_BLOCK
        head_row = lax.broadcasted_iota(jnp.int32, (2 * ATTN_BLOCK, 1), 0) < ATTN_BLOCK
        for j in range(n_qblk):
            bias = jnp.where(band_ok, 0.0, NEG_INF)
            if j == 0:
                lo = jnp.where(is_first, ATTN_BLOCK, 0)
                bias = jnp.where(c_i >= lo, bias, NEG_INF)
            if j == n_qblk - 1:
                hi = jnp.where(is_last_tile, 2 * ATTN_BLOCK, 3 * ATTN_BLOCK)
                bias = jnp.where(c_i < hi, bias, NEG_INF)
            js = slice(j * ATTN_BLOCK, (j + 1) * ATTN_BLOCK)
            ks = slice(j * ATTN_BLOCK, (j + 3) * ATTN_BLOCK)
            for g in range(N_KV_HEADS):
                q2 = jnp.concatenate([q_scr[2 * g, js, :], q_scr[2 * g + 1, js, :]], axis=0)
                s_loc = _dot_nt(q2, k_scr[g, ks, :]) + bias
                s_ctx = _dot_nt(q2, ck_bf[:, g * HEAD_DIM:(g + 1) * HEAD_DIM])
                sink = jnp.where(head_row, r.sink[layer, 2 * g] * LOG2_E, r.sink[layer, 2 * g + 1] * LOG2_E)
                m = jnp.maximum(jnp.maximum(jnp.max(s_loc, axis=-1, keepdims=True),
                                            jnp.max(s_ctx, axis=-1, keepdims=True)), sink)
                yield
                p_loc = jnp.exp2(s_loc - m)
                p_ctx = jnp.exp2(s_ctx - m)
                den = (jnp.sum(p_loc, axis=-1, keepdims=True) + jnp.sum(p_ctx, axis=-1, keepdims=True)
                       + jnp.exp2(sink - m))
                yield
                o = (_dot(p_loc.astype(jnp.bfloat16), v_scr[g, ks, :])
                     + _dot(p_ctx.astype(jnp.bfloat16), cv_bf[:, g * HEAD_DIM:(g + 1) * HEAD_DIM]))
                o = (o / den).astype(jnp.bfloat16)
                mix_scr[js, (2 * g) * HEAD_DIM:(2 * g + 1) * HEAD_DIM] = o[:ATTN_BLOCK]
                mix_scr[js, (2 * g + 1) * HEAD_DIM:(2 * g + 2) * HEAD_DIM] = o[ATTN_BLOCK:]
                yield
    else:
        head_row = lax.broadcasted_iota(jnp.int32, (2 * tile, 1), 0) < tile
        for q in range(n_seq):
            qs = slice(q * tile, (q + 1) * tile)
            for g in range(N_KV_HEADS):
                q2 = jnp.concatenate([q_scr[2 * g, qs, :], q_scr[2 * g + 1, qs, :]], axis=0)
                s = _dot_nt(q2, k_scr[g, qs, :])
                sink = jnp.where(head_row, r.sink[layer, 2 * g] * LOG2_E, r.sink[layer, 2 * g + 1] * LOG2_E)
                m = jnp.maximum(jnp.max(s, axis=-1, keepdims=True), sink)
                yield
                p = jnp.exp2(s - m)
                den = jnp.sum(p, axis=-1, keepdims=True) + jnp.exp2(sink - m)
                yield
                o = (_dot(p.astype(jnp.bfloat16), v_scr[g, qs, :]) / den).astype(jnp.bfloat16)
                mix_scr[qs, (2 * g) * HEAD_DIM:(2 * g + 1) * HEAD_DIM] = o[:tile]
                mix_scr[qs, (2 * g + 1) * HEAD_DIM:(2 * g + 2) * HEAD_DIM] = o[tile:]
                yield

    srows = tile + 2 * MARGIN
    lo_grp = lax.broadcasted_iota(jnp.int32, (1, 128), 1) < POOL_W // POOL_GROUPS
    pool_cols = slice(ATTN_W, ATTN_W + POOL_W)
    conv_cols = slice(ATTN_W + POOL_W, ATTN_W + POOL_W + CONV_W)
    first_tap = MARGIN - CONV_HALF

    for q in range(n_seq):
        base = q * rows
        qs = slice(q * tile, (q + 1) * tile)

        def fill_slab(fn):
            slab_scr[MARGIN:MARGIN + tile, :] = fn(base + c0, base + c1)
            zeros = jnp.zeros((MARGIN, 256), jnp.float32)
            if halo:
                left = fn(c0 - MARGIN, c0)
                right = fn(c1, c1 + MARGIN)
                slab_scr[0:MARGIN, :] = jnp.where(is_first, zeros, left)
                slab_scr[MARGIN + tile:srows, :] = jnp.where(is_last_tile, zeros, right)
            else:
                slab_scr[0:MARGIN, :] = zeros
                slab_scr[MARGIN + tile:srows, :] = zeros

        fill_slab(lambda a, b: proj_scr[a:b, C_POOL:C_POOL + POOL_W])
        for hf in range(2):
            xs = slab_scr[:, hf * 128:(hf + 1) * 128]
            size_lo, size_hi = POOL_SIZES[2 * hf], POOL_SIZES[2 * hf + 1]
            wins = {}
            acc = xs
            shift_by = 1
            for size in POOL_SIZES:
                if size > size_hi:
                    break
                acc = acc + pltpu.roll(acc, shift_by, 0)
                shift_by = size
                if size in (size_lo, size_hi):
                    shift = size // 2 - 1
                    win = acc if shift == 0 else pltpu.roll(acc, srows - shift, 0)
                    wins[size] = win[MARGIN:MARGIN + tile]
            wsum = jnp.where(lo_grp, wins[size_lo], wins[size_hi])
            half_w = jnp.where(lo_grp, size_lo // 2, size_hi // 2)
            inv_full = jnp.where(lo_grp, 1.0 / size_lo, 1.0 / size_hi)

            def edge(row0):
                tpos = ti * tile + row0 + lax.broadcasted_iota(jnp.int32, (POOL_EDGE, 1), 0)
                cnt = jnp.minimum(tpos + half_w, seq_len) - jnp.maximum(tpos - half_w, 0)
                return wsum[row0:row0 + POOL_EDGE] / cnt.astype(jnp.float32)

            mean = jnp.concatenate(
                [edge(0), wsum[POOL_EDGE:tile - POOL_EDGE] * inv_full, edge(tile - POOL_EDGE)], axis=0)
            pooled = mean - xs[MARGIN:MARGIN + tile]
            mix_scr[qs, ATTN_W + hf * 128:ATTN_W + (hf + 1) * 128] = pooled.astype(jnp.bfloat16)
        yield

        def glu(a, b):
            av = proj_scr[a:b, C_CONV:C_CONV + CONV_W]
            gv = proj_scr[a:b, C_CONV + CONV_W:C_CONV + 2 * CONV_W]
            return av * jax.nn.sigmoid(gv)

        fill_slab(glu)
        for blk in range(tile // CONV_ROWS):
            b0 = blk * CONV_ROWS
            y = None
            for rr in range(8):
                z = None
                for a in range((first_tap + CONV_WIDTH + 7) // 8):
                    o = 8 * a + rr
                    if first_tap <= o < first_tap + CONV_WIDTH:
                        j = o - first_tap
                        term = slab_scr[b0 + 8 * a:b0 + 8 * a + CONV_ROWS + 8, :] * r.conv_dw[0, j:j + 1, :]
                        z = term if z is None else z + term
                zr = z[rr:rr + CONV_ROWS]
                y = zr if y is None else y + zr
            y = _rms(y + r.conv_b[0], r.conv_norm[0])
            y = y * jax.nn.sigmoid(y)
            mix_scr[q * tile + b0:q * tile + b0 + CONV_ROWS, conv_cols] = y.astype(jnp.bfloat16)
            yield

    pool_out = _dot(mix_scr[:, pool_cols], r.pool_w[0]) * r.pool_scale[0]
    mix_scr[:, pool_cols] = pool_out.astype(jnp.bfloat16)
    conv_out = _dot(mix_scr[:, conv_cols], r.conv_pw[0])
    mix_scr[:, conv_cols] = conv_out.astype(jnp.bfloat16)
    yield

    gm_b = r.gm_b[0]
    for q in range(n_seq):
        pr = slice(q * rows + c0, q * rows + c1)
        gu = jax.nn.gelu(proj_scr[pr, C_GM:C_GM + GM_W])
        gv = jax.nn.gelu(proj_scr[pr, C_GM + GM_W:C_GM + 2 * GM_W])
        gv = _rms(gv, r.gm_norm[0]).astype(jnp.bfloat16)
        for c in range(tile // GM_CHUNK):
            cs = slice(c * GM_CHUNK, (c + 1) * GM_CHUNK)
            parts = []
            for g in range(GM_GROUPS):
                parts.append(_dot(r.gm_ws[0, g], gv[cs, g * GM_GROUP_W:(g + 1) * GM_GROUP_W]))
            sv = jnp.concatenate(parts, axis=-1) + gm_b
            ms = slice(q * tile + c * GM_CHUNK, q * tile + (c + 1) * GM_CHUNK)
            mix_scr[ms, ATTN_W + POOL_W + CONV_W:D_MODEL] = (gu[cs] * sv).astype(jnp.bfloat16)
        yield

    half = n_seq * tile // 2
    for hq in range(2):
        mixed = _dot(mix_scr[hq * half:(hq + 1) * half, :], r.w_out[0])
        for lo in range(0, half, TAIL_ROWS):
            row = hq * half + lo
            q, t0 = row // tile, row % tile
            x1 = r.x[q, t0:t0 + TAIL_ROWS, :] + g1 * mixed[lo:lo + TAIL_ROWS]
            r.x1_scr[slot, row:row + TAIL_ROWS, :] = x1
            r.h2_scr[slot, row:row + TAIL_ROWS, :] = (_rms(x1, norm2) * (1.0 + sc2) + sh2).astype(jnp.bfloat16)
            yield


def _mixer_pieces(cfg):
    attn = (cfg.tile // ATTN_BLOCK if cfg.latent else cfg.n_seq) * N_KV_HEADS
    return 1 + 3 * attn +cfg.n_seq * (1 + cfg.tile // CONV_ROWS) + 1 + cfg.n_seq


def _layer_kernel(*refs, cfg):
    names = []
    if cfg.halo:
        names += ['xprev', 'x', 'xnext']
    else:
        names += ['x']
    names += ['mod']
    if cfg.latent:
        names += ['modp']
    names += ['norm1', 'norm2', 'w_in', 'w_out', 'sink', 'pool_w', 'pool_scale', 'conv_dw', 'conv_b', 'conv_norm',
              'conv_pw', 'gm_norm', 'gm_ws', 'gm_b', 'w1', 'w2']
    if cfg.latent:
        names += ['cos', 'sin', 'ck', 'cv']
    if cfg.last:
        names += ['fnorm']
    if not cfg.latent:
        names += ['kacc', 'vacc']
    names += ['out']
    if not cfg.latent:
        names += ['kout', 'vout']
    names += ['h_scr', 'proj_scr', 'q_scr', 'k_scr', 'v_scr', 'slab_scr', 'mix_scr', 'acc_scr', 'x1_scr', 'h2_scr']
    assert len(names) == len(refs)
    r = types.SimpleNamespace(**dict(zip(names, refs)))
    if not cfg.latent:
        r.modp = r.mod

    crows = cfg.n_seq * cfg.tile
    step = pl.program_id(0)

    @pl.when(step == 0)
    def _():
        r.x1_scr[1] = jnp.zeros((crows, D_MODEL), jnp.float32)
        r.h2_scr[1] = jnp.zeros((crows, D_MODEL), jnp.bfloat16)

    @pl.when(step == cfg.n_steps)
    def _():
        for _ in _mlp_half(r, cfg):
            pass

    @pl.when(step < cfg.n_steps)
    def _():
        mlp = _mlp_half(r, cfg)
        mixer = _mixer_half(r, cfg)
        n_mlp = 2 * (D_FF // FF_CHUNK)
        n_mix = _mixer_pieces(cfg)
        tail_mlp = 3
        next(mlp)
        next(mixer)
        done = 1
        for i in range(n_mix):
            next(mixer)
            target = ((i + 1) * (n_mlp - tail_mlp)) // n_mix
            while done < target:
                next(mlp)
                done += 1
        n_tail = crows // TAIL_ROWS
        for k, _ in enumerate(mixer):
            target = n_mlp - tail_mlp + ((k + 1) * tail_mlp) // n_tail
            while done < min(target, n_mlp):
                next(mlp)
                done += 1
        for _ in mlp:
            pass


def _const_spec(shape, index):
    return pl.BlockSpec(shape, lambda s: index, pipeline_mode=pl.Buffered(1))


def _layer_call(x, layer, mod_all, params, *, tile, latent, last, rope=None, cache=None, final_norm=None,
                kv_acc=None):
    batch, seq_len, _ = x.shape
    halo = HALO if latent else 0
    rows = tile + 2 * halo
    n_tiles = seq_len // tile
    n_seq = STEP_ROWS // tile
    assert seq_len % tile == 0 and tile % ATTN_BLOCK == 0 and STEP_ROWS % tile == 0
    if latent:
        assert n_seq == 1
    else:
        assert n_tiles == 1 and batch % n_seq == 0
    n_steps = (batch // n_seq) * n_tiles
    last_step = n_steps - 1

    def cur(s):
        return jnp.minimum(s, last_step)

    def prev(s):
        return jnp.maximum(s - 1, 0)

    in_specs = []
    args = []
    if halo:
        hb = tile // halo
        n_hblk = seq_len // halo
        in_specs.append(pl.BlockSpec(
            (1, halo, D_MODEL),
            lambda s: (cur(s) // n_tiles, jnp.maximum((cur(s) % n_tiles) * hb - 1, 0), 0)))
        args.append(x)
    in_specs.append(pl.BlockSpec((n_seq, tile, D_MODEL), lambda s: (cur(s) // n_tiles, cur(s) % n_tiles, 0)))
    args.append(x)
    if halo:
        in_specs.append(pl.BlockSpec(
            (1, halo, D_MODEL),
            lambda s: (cur(s) // n_tiles, jnp.minimum((cur(s) % n_tiles + 1) * hb, n_hblk - 1), 0)))
        args.append(x)

    if latent:
        in_specs.append(pl.BlockSpec((1, 1, N_MOD * D_MODEL),
                                     lambda s: (layer * MOD_ROWS + cur(s) // n_tiles, 0, 0)))
        args.append(mod_all)
        in_specs.append(pl.BlockSpec((1, 1, N_MOD * D_MODEL),
                                     lambda s: (layer * MOD_ROWS + prev(s) // n_tiles, 0, 0)))
        args.append(mod_all)
    else:
        in_specs.append(pl.BlockSpec((1, 1, N_MOD * D_MODEL), lambda s: (layer * MOD_ROWS + CTX_MOD_ROW, 0, 0)))
        args.append(mod_all)

    def add_const(arr, block):
        idx = (layer,) + (0,) * (len(block) - 1)
        in_specs.append(_const_spec(block, idx))
        args.append(arr)

    add_const(params['norm1'], (1, 1, D_MODEL))
    add_const(params['norm2'], (1, 1, D_MODEL))
    add_const(params['w_in'], (1, D_MODEL, IN_W))
    add_const(params['w_out'], (1, D_MODEL, D_MODEL))
    in_specs.append(pl.BlockSpec(memory_space=pltpu.SMEM))
    args.append(params['attn_sink'])
    add_const(params['pool_w'], (1, POOL_W, POOL_W))
    add_const(params['pool_scale'], (1, 1, POOL_W))
    add_const(params['conv_dw'], (1, CONV_WIDTH, CONV_W))
    add_const(params['conv_b'], (1, 1, CONV_W))
    add_const(params['conv_norm'], (1, 1, CONV_W))
    add_const(params['conv_pw'], (1, CONV_W, CONV_W))
    add_const(params['gm_norm'], (1, 1, GM_W))
    add_const(params['gm_ws'], (1, GM_GROUPS, GM_CHUNK, GM_CHUNK))
    add_const(params['gm_b'], (1, GM_CHUNK, GM_W))
    add_const(params['w_mlp1'], (1, D_MODEL, D_FF))
    add_const(params['w_mlp2'], (1, D_FF, D_MODEL))
    if latent:
        cos_t, sin_t = rope
        in_specs.append(_const_spec(cos_t.shape, (0, 0)))
        args.append(cos_t)
        in_specs.append(_const_spec(sin_t.shape, (0, 0)))
        args.append(sin_t)
        ck, cv = cache
        past = ck.shape[2]
        in_specs.append(pl.BlockSpec((1, 1, past, KV_W), lambda s: (cur(s) // n_tiles, layer, 0, 0)))
        args.append(ck)
        in_specs.append(pl.BlockSpec((1, 1, past, KV_W), lambda s: (cur(s) // n_tiles, layer, 0, 0)))
        args.append(cv)
    if last:
        in_specs.append(_const_spec((1, D_MODEL), (0, 0)))
        args.append(final_norm)

    out_shape = [jax.ShapeDtypeStruct(x.shape, jnp.float32)]
    out_specs = [pl.BlockSpec((n_seq, tile, D_MODEL), lambda s: (prev(s) // n_tiles, prev(s) % n_tiles, 0))]
    aliases = {}
    if not latent:
        for i, acc in enumerate(kv_acc):
            in_specs.append(pl.BlockSpec(memory_space=pl.ANY))
            args.append(acc)
            aliases[len(args) - 1] = 1 + i
            out_shape.append(jax.ShapeDtypeStruct(acc.shape, jnp.float32))
            out_specs.append(pl.BlockSpec((n_seq, 1, tile, KV_W), lambda s: (cur(s), layer, 0, 0)))

    crows = n_seq * tile
    scratch = [
        pltpu.VMEM((n_seq * rows, D_MODEL), jnp.bfloat16),
        pltpu.VMEM((n_seq * rows, IN_W), jnp.float32),
        pltpu.VMEM((N_Q_HEADS, crows, HEAD_DIM), jnp.bfloat16),
        pltpu.VMEM((N_KV_HEADS, n_seq * rows, HEAD_DIM), jnp.bfloat16),
        pltpu.VMEM((N_KV_HEADS, n_seq * rows, HEAD_DIM), jnp.bfloat16),
        pltpu.VMEM((tile + 2 * MARGIN, 256), jnp.float32),
        pltpu.VMEM((crows, D_MODEL), jnp.bfloat16),
        pltpu.VMEM((crows, D_MODEL), jnp.float32),
        pltpu.VMEM((2, crows, D_MODEL), jnp.float32),
        pltpu.VMEM((2, crows, D_MODEL), jnp.bfloat16),
    ]
    cfg = types.SimpleNamespace(layer=layer, tile=tile, n_seq=n_seq, halo=halo, seq_len=seq_len, n_steps=n_steps,
                                latent=latent, last=last)
    kern = functools.partial(_layer_kernel, cfg=cfg)
    outs = pl.pallas_call(
        kern,
        grid=(n_steps + 1,),
        in_specs=in_specs,
        out_specs=out_specs,
        out_shape=out_shape,
        input_output_aliases=aliases,
        scratch_shapes=scratch,
        compiler_params=pltpu.CompilerParams(
            dimension_semantics=("arbitrary",),
            vmem_limit_bytes=VMEM_LIMIT_BYTES),
        name=("latent" if latent else "context") + "_layer",
    )(*args)
    return outs


def _rope_tables(seq_len):
    rows = seq_len // GRID_W
    row = np.repeat(np.arange(rows), GRID_W).astype(np.float32)
    col = np.tile(np.arange(GRID_W), rows).astype(np.float32)
    inv = (np.float32(ROPE_BASE) ** (-np.arange(ROPE_PAIRS, dtype=np.float32) / np.float32(ROPE_PAIRS))).astype(
        np.float32)
    ang = np.concatenate([row[:, None] * inv, col[:, None] * inv], axis=-1)
    cos, sin = np.cos(ang.astype(np.float64)), np.sin(ang.astype(np.float64))
    cos_t = np.tile(np.concatenate([cos, cos], axis=-1), (1, 2)).astype(np.float32)
    sin_t = np.tile(np.concatenate([-sin, sin], axis=-1), (1, 2)).astype(np.float32)
    pad = ((HALO, HALO), (0, 0))
    return jnp.asarray(np.pad(cos_t, pad)), jnp.asarray(np.pad(sin_t, pad))


def _prep_params(norm1, norm2, w_in, w_out, attn_sink, pool_w, pool_scale, conv_dw, conv_b, conv_norm,
                 conv_pw, gm_norm, gm_ws, gm_b, w_mlp1, w_mlp2):
    depth = w_in.shape[0]
    bf = jnp.bfloat16
    gw = POOL_W // POOL_GROUPS
    pool_bd = jnp.zeros((depth, POOL_W, POOL_W), jnp.float32)
    for g in range(POOL_GROUPS):
        pool_bd = pool_bd.at[:, g * gw:(g + 1) * gw, g * gw:(g + 1) * gw].set(pool_w[:, g])
    gm_b_full = jnp.repeat(jnp.transpose(gm_b, (0, 2, 1)), GM_GROUP_W, axis=-1)
    return {
        'norm1': norm1.reshape(depth, 1, D_MODEL),
        'norm2': norm2.reshape(depth, 1, D_MODEL),
        'w_in': w_in.astype(bf),
        'w_out': w_out.astype(bf),
        'attn_sink': attn_sink,
        'pool_w': pool_bd.astype(bf),
        'pool_scale': pool_scale.reshape(depth, 1, POOL_W),
        'conv_dw': conv_dw,
        'conv_b': conv_b.reshape(depth, 1, CONV_W),
        'conv_norm': conv_norm.reshape(depth, 1, CONV_W),
        'conv_pw': conv_pw.astype(bf),
        'gm_norm': gm_norm.reshape(depth, 1, GM_W),
        'gm_ws': gm_ws.astype(bf),
        'gm_b': gm_b_full,
        'w_mlp1': w_mlp1.astype(bf),
        'w_mlp2': w_mlp2.astype(bf),
    }


def kernel(x_prompt, x_sample, cache_k, cache_v, c, c_ctx, w_ada, b_ada, norm1, norm2, w_in, w_out, attn_sink,
           pool_w, pool_scale, conv_dw, conv_b, conv_norm, conv_pw, gm_norm, gm_ws, gm_b, w_mlp1, w_mlp2,
           final_norm):
    depth = w_in.shape[0]
    dec_batch, dec_seq, _ = x_sample.shape
    batch, seq, _ = x_prompt.shape
    past = cache_k.shape[2]

    cond = jnp.zeros((MOD_ROWS, D_MODEL), jnp.float32)
    cond = cond.at[:dec_batch].set(c).at[CTX_MOD_ROW].set(c_ctx)
    mod_all = _ada_mod(cond, w_ada, b_ada).reshape(depth * MOD_ROWS, 1, N_MOD * D_MODEL)

    params = _prep_params(norm1, norm2, w_in, w_out, attn_sink, pool_w, pool_scale, conv_dw, conv_b,
                          conv_norm, conv_pw, gm_norm, gm_ws, gm_b, w_mlp1, w_mlp2)
    rope = _rope_tables(dec_seq)
    ck = cache_k.reshape(dec_batch, depth, past, KV_W)
    cv = cache_v.reshape(dec_batch, depth, past, KV_W)
    fnorm = final_norm.reshape(1, D_MODEL)

    xc, xs = x_prompt, x_sample
    new_k = jnp.zeros((batch, depth, seq, KV_W), jnp.float32)
    new_v = jnp.zeros((batch, depth, seq, KV_W), jnp.float32)
    for l in range(depth):
        last = l == depth - 1
        xc, new_k, new_v = _layer_call(xc, l, mod_all, params, tile=seq, latent=False, last=last,
                                       final_norm=fnorm if last else None, kv_acc=(new_k, new_v))
        (xs,) = _layer_call(xs, l, mod_all, params, tile=STEP_ROWS, latent=True, last=last,
                            rope=rope, cache=(ck, cv), final_norm=fnorm if last else None)
    new_k = new_k.reshape(batch, depth, seq, N_KV_HEADS, HEAD_DIM)
    new_v = new_v.reshape(batch, depth, seq, N_KV_HEADS, HEAD_DIM)
    return (xc, xs, new_k, new_v)
```

```python
import functools
import types

import jax
import jax.numpy as jnp
import numpy as np
from jax import lax
from jax.experimental import pallas as pl
from jax.experimental.pallas import tpu as pltpu

D_MODEL = 1024
DEPTH = 4
ATTN_W = 256
POOL_W = 256
CONV_W = 256
GM_W = 256
HEAD_DIM = 64
N_Q_HEADS = 4
N_KV_HEADS = 2
Q_PER_KV = 2
KV_W = 128
ATTN_BLOCK = 128
GRID_W = 64
ROPE_BASE = 10000.0
ROPE_PAIRS = 16
POOL_GROUPS = 4
POOL_SIZES = (2, 4, 8, 16)
CONV_WIDTH = 31
CONV_HALF = 15
GM_GROUPS = 4
GM_GROUP_W = 64
GM_CHUNK = 128
D_FF = 4096
N_MOD = 6
EPS = 1e-6
NEG_INF = -1e30
LOG2_E = 1.4426950408889634
IN_W = 1792
C_Q, C_K, C_V, C_POOL, C_CONV, C_GM = 0, 256, 384, 512, 768, 1280

HALO = 128
MARGIN = 16
POOL_EDGE = 16
MOD_ROWS = 16
CTX_MOD_ROW = 8
FF_CHUNK = 512
CONV_ROWS = 64
TAIL_ROWS = 128
STEP_ROWS = 512
VMEM_LIMIT_BYTES = 58 * 1024 * 1024


def _rms(x, g):
    ms = jnp.mean(x * x, axis=-1, keepdims=True)
    return x * lax.rsqrt(ms + EPS) * g


def _dot(a, b):
    return jnp.dot(a, b, preferred_element_type=jnp.float32)


def _dot_nt(a, b):
    return lax.dot_general(a, b, (((1,), (1,)), ((), ())), preferred_element_type=jnp.float32)


def _ada_kernel(cond_ref, w_ref, b_ref, o_ref):
    s = cond_ref[...]
    s = s * jax.nn.sigmoid(s)
    o_ref[0] = _dot(s.astype(jnp.bfloat16), w_ref[0].astype(jnp.bfloat16)) + b_ref[0]


def _ada_mod(cond, w_ada, b_ada):
    tn = 1536
    nt = (N_MOD * D_MODEL) // tn
    return pl.pallas_call(
        _ada_kernel,
        grid=(DEPTH, nt),
        in_specs=[
            pl.BlockSpec((MOD_ROWS, D_MODEL), lambda l, j: (0, 0)),
            pl.BlockSpec((1, D_MODEL, tn), lambda l, j: (l, 0, j)),
            pl.BlockSpec((1, 1, tn), lambda l, j: (l, 0, j)),
        ],
        out_specs=pl.BlockSpec((1, MOD_ROWS, tn), lambda l, j: (l, 0, j)),
        out_shape=jax.ShapeDtypeStruct((DEPTH, MOD_ROWS, N_MOD * D_MODEL), jnp.float32),
        compiler_params=pltpu.CompilerParams(
            dimension_semantics=("arbitrary", "arbitrary"),
            vmem_limit_bytes=VMEM_LIMIT_BYTES),
        name="ada_mod",
    )(cond, w_ada, b_ada.reshape(DEPTH, 1, N_MOD * D_MODEL))


def _mlp_half(r, cfg):
    tile, n_seq = cfg.tile, cfg.n_seq
    slot = 1 - pl.program_id(0) % 2
    g2 = r.modp[0][:, 5 * D_MODEL:6 * D_MODEL]
    n_chunks = D_FF // FF_CHUNK
    for c in range(n_chunks):
        a = _dot(r.h2_scr[slot], r.w1[0, :, c * FF_CHUNK:(c + 1) * FF_CHUNK])
        a = jnp.square(jnp.maximum(a, 0.0)).astype(jnp.bfloat16)
        yield
        w2_c = r.w2[0, c * FF_CHUNK:(c + 1) * FF_CHUNK, :]
        if c < n_chunks - 1:
            f = _dot(a, w2_c)
            if c == 0:
                r.acc_scr[...] = f
            else:
                r.acc_scr[...] += f
            yield
    half = n_seq * tile // 2
    for hq in range(2):
        hs = slice(hq * half, (hq + 1) * half)
        x2 = r.x1_scr[slot, hs, :] + g2 * (r.acc_scr[hs, :] + _dot(a[hs], w2_c))
        if cfg.last:
            x2 = _rms(x2, r.fnorm[...])
        q, t0 = (hq * half) // tile, (hq * half) % tile
        r.out[q, t0:t0 + half, :] = x2
        if hq == 0:
            yield


def _mixer_half(r, cfg):
    tile, n_seq, halo, seq_len, latent, layer = cfg.tile, cfg.n_seq, cfg.halo, cfg.seq_len, cfg.latent, cfg.layer
    rows = tile + 2 * halo
    step = pl.program_id(0)
    slot = step % 2
    cur = jnp.minimum(step, cfg.n_steps - 1)
    n_tiles = seq_len // tile
    ti = cur % n_tiles
    is_first = ti == 0
    is_last_tile = ti == n_tiles - 1
    h_scr, proj_scr, q_scr, k_scr, v_scr, slab_scr, mix_scr = (
        r.h_scr, r.proj_scr, r.q_scr, r.k_scr, r.v_scr, r.slab_scr, r.mix_scr)

    mod = r.mod[0]
    sh1 = mod[:, 0 * D_MODEL:1 * D_MODEL]
    sc1 = mod[:, 1 * D_MODEL:2 * D_MODEL]
    g1 = mod[:, 2 * D_MODEL:3 * D_MODEL]
    sh2 = mod[:, 3 * D_MODEL:4 * D_MODEL]
    sc2 = mod[:, 4 * D_MODEL:5 * D_MODEL]
    norm1 = r.norm1[0]
    norm2 = r.norm2[0]

    def norm_mod1(x):
        return (_rms(x, norm1) * (1.0 + sc1) + sh1).astype(jnp.bfloat16)

    c0 = halo
    c1 = halo + tile
    if halo:
        h_scr[0:halo, :] = norm_mod1(r.xprev[0])
        h_scr[c1:rows, :] = norm_mod1(r.xnext[0])
        h_scr[c0:c1, :] = norm_mod1(r.x[0])
        m0, m1 = c0 - MARGIN, c1 + MARGIN
        proj_scr[c0:c1, C_Q:C_K] = _dot(h_scr[c0:c1, :], r.w_in[0, :, C_Q:C_K])
        proj_scr[:, C_K:C_POOL] = _dot(h_scr[...], r.w_in[0, :, C_K:C_POOL])
        proj_scr[m0:m1, C_POOL:C_GM] = _dot(h_scr[m0:m1, :], r.w_in[0, :, C_POOL:C_GM])
        proj_scr[c0:c1, C_GM:IN_W] = _dot(h_scr[c0:c1, :], r.w_in[0, :, C_GM:IN_W])
    else:
        for q in range(n_seq):
            h_scr[q * tile:(q + 1) * tile, :] = norm_mod1(r.x[q])
        proj_scr[...] = _dot(h_scr[...], r.w_in[0])
    yield

    scale = HEAD_DIM ** -0.5 * LOG2_E
    if latent:
        row0 = ti * tile
        cos_t = r.cos[pl.ds(pl.multiple_of(row0, tile), rows), :]
        sin_t = r.sin[pl.ds(pl.multiple_of(row0, tile), rows), :]
        lane = lax.broadcasted_iota(jnp.int32, (1, 128), 1)
        first_half = (lane % HEAD_DIM) < (HEAD_DIM // 2)

        def rope(z, cs, sn):
            swapped = jnp.where(first_half, pltpu.roll(z, 96, 1), pltpu.roll(z, 32, 1))
            return z * cs + swapped * sn

    k_all = proj_scr[:, C_K:C_K + KV_W]
    if not latent:
        for q in range(n_seq):
            r.kout[q, 0] = k_all[q * tile:(q + 1) * tile]
            r.vout[q, 0] = proj_scr[q * tile:(q + 1) * tile, C_V:C_V + KV_W]
    if latent:
        k_all = rope(k_all, cos_t, sin_t)
    k_bf = k_all.astype(jnp.bfloat16)
    v_bf = proj_scr[:, C_V:C_V + KV_W].astype(jnp.bfloat16)
    for g in range(N_KV_HEADS):
        k_scr[g] = k_bf[:, g * HEAD_DIM:(g + 1) * HEAD_DIM]
        v_scr[g] = v_bf[:, g * HEAD_DIM:(g + 1) * HEAD_DIM]
    for q in range(n_seq):
        for half in range(2):
            qh = proj_scr[q * rows + c0:q * rows + c1, C_Q + half * 128:C_Q + (half + 1) * 128]
            if latent:
                qh = rope(qh, cos_t[c0:c1], sin_t[c0:c1])
            qh = (qh * scale).astype(jnp.bfloat16)
            for s in range(2):
                q_scr[2 * half + s, q * tile:(q + 1) * tile, :] = qh[:, s * HEAD_DIM:(s + 1) * HEAD_DIM]
    yield

    if latent:
        ck_bf = r.ck[0, 0].astype(jnp.bfloat16)
        cv_bf = r.cv[0, 0].astype(jnp.bfloat16)
        r_i = lax.broadcasted_iota(jnp.int32, (2 * ATTN_BLOCK, 3 * ATTN_BLOCK), 0) % ATTN_BLOCK
        c_i = lax.broadcasted_iota(jnp.int32, (2 * ATTN_BLOCK, 3 * ATTN_BLOCK), 1)
        band_ok = jnp.abs(r_i - (c_i - ATTN_BLOCK)) <= HALO
        n_qblk = tile // ATTN_BLOCK
        head_row = lax.broadcasted_iota(jnp.int32, (2 * ATTN_BLOCK, 1), 0) < ATTN_BLOCK
        for j in range(n_qblk):
            bias = jnp.where(band_ok, 0.0, NEG_INF)
            if j == 0:
                lo = jnp.where(is_first, ATTN_BLOCK, 0)
                bias = jnp.where(c_i >= lo, bias, NEG_INF)
            if j == n_qblk - 1:
                hi = jnp.where(is_last_tile, 2 * ATTN_BLOCK, 3 * ATTN_BLOCK)
                bias = jnp.where(c_i < hi, bias, NEG_INF)
            js = slice(j * ATTN_BLOCK, (j + 1) * ATTN_BLOCK)
            ks = slice(j * ATTN_BLOCK, (j + 3) * ATTN_BLOCK)
            for g in range(N_KV_HEADS):
                q2 = jnp.concatenate([q_scr[2 * g, js, :], q_scr[2 * g + 1, js, :]], axis=0)
                s_loc = _dot_nt(q2, k_scr[g, ks, :]) + bias
                s_ctx = _dot_nt(q2, ck_bf[:, g * HEAD_DIM:(g + 1) * HEAD_DIM])
                sink = jnp.where(head_row, r.sink[layer, 2 * g] * LOG2_E, r.sink[layer, 2 * g + 1] * LOG2_E)
                m = jnp.maximum(jnp.maximum(jnp.max(s_loc, axis=-1, keepdims=True),
                                            jnp.max(s_ctx, axis=-1, keepdims=True)), sink)
                yield
                p_loc = jnp.exp2(s_loc - m)
                p_ctx = jnp.exp2(s_ctx - m)
                den = (jnp.sum(p_loc, axis=-1, keepdims=True) + jnp.sum(p_ctx, axis=-1, keepdims=True)
                       + jnp.exp2(sink - m))
                yield
                o = (_dot(p_loc.astype(jnp.bfloat16), v_scr[g, ks, :])
                     + _dot(p_ctx.astype(jnp.bfloat16), cv_bf[:, g * HEAD_DIM:(g + 1) * HEAD_DIM]))
                o = (o / den).astype(jnp.bfloat16)
                mix_scr[js, (2 * g) * HEAD_DIM:(2 * g + 1) * HEAD_DIM] = o[:ATTN_BLOCK]
                mix_scr[js, (2 * g + 1) * HEAD_DIM:(2 * g + 2) * HEAD_DIM] = o[ATTN_BLOCK:]
                yield
    else:
        head_row = lax.broadcasted_iota(jnp.int32, (2 * tile, 1), 0) < tile
        for q in range(n_seq):
            qs = slice(q * tile, (q + 1) * tile)
            for g in range(N_KV_HEADS):
                q2 = jnp.concatenate([q_scr[2 * g, qs, :], q_scr[2 * g + 1, qs, :]], axis=0)
                s = _dot_nt(q2, k_scr[g, qs, :])
                sink = jnp.where(head_row, r.sink[layer, 2 * g] * LOG2_E, r.sink[layer, 2 * g + 1] * LOG2_E)
                m = jnp.maximum(jnp.max(s, axis=-1, keepdims=True), sink)
                yield
                p = jnp.exp2(s - m)
                den = jnp.sum(p, axis=-1, keepdims=True) + jnp.exp2(sink - m)
                yield
                o = (_dot(p.astype(jnp.bfloat16), v_scr[g, qs, :]) / den).astype(jnp.bfloat16)
                mix_scr[qs, (2 * g) * HEAD_DIM:(2 * g + 1) * HEAD_DIM] = o[:tile]
                mix_scr[qs, (2 * g + 1) * HEAD_DIM:(2 * g + 2) * HEAD_DIM] = o[tile:]
                yield

    srows = tile + 2 * MARGIN
    lo_grp = lax.broadcasted_iota(jnp.int32, (1, 128), 1) < POOL_W // POOL_GROUPS
    pool_cols = slice(ATTN_W, ATTN_W + POOL_W)
    conv_cols = slice(ATTN_W + POOL_W, ATTN_W + POOL_W + CONV_W)
    first_tap = MARGIN - CONV_HALF

    for q in range(n_seq):
        base = q * rows
        qs = slice(q * tile, (q + 1) * tile)

        def fill_slab(fn):
            slab_scr[MARGIN:MARGIN + tile, :] = fn(base + c0, base + c1)
            zeros = jnp.zeros((MARGIN, 256), jnp.float32)
            if halo:
                left = fn(c0 - MARGIN, c0)
                right = fn(c1, c1 + MARGIN)
                slab_scr[0:MARGIN, :] = jnp.where(is_first, zeros, left)
                slab_scr[MARGIN + tile:srows, :] = jnp.where(is_last_tile, zeros, right)
            else:
                slab_scr[0:MARGIN, :] = zeros
                slab_scr[MARGIN + tile:srows, :] = zeros

        fill_slab(lambda a, b: proj_scr[a:b, C_POOL:C_POOL + POOL_W])
        for hf in range(2):
            xs = slab_scr[:, hf * 128:(hf + 1) * 128]
            size_lo, size_hi = POOL_SIZES[2 * hf], POOL_SIZES[2 * hf + 1]
            wins = {}
            acc = xs
            shift_by = 1
            for size in POOL_SIZES:
                if size > size_hi:
                    break
                acc = acc + pltpu.roll(acc, shift_by, 0)
                shift_by = size
                if size in (size_lo, size_hi):
                    shift = size // 2 - 1
                    win = acc if shift == 0 else pltpu.roll(acc, srows - shift, 0)
                    wins[size] = win[MARGIN:MARGIN + tile]
            wsum = jnp.where(lo_grp, wins[size_lo], wins[size_hi])
            half_w = jnp.where(lo_grp, size_lo // 2, size_hi // 2)
            inv_full = jnp.where(lo_grp, 1.0 / size_lo, 1.0 / size_hi)

            def edge(row0):
                tpos = ti * tile + row0 + lax.broadcasted_iota(jnp.int32, (POOL_EDGE, 1), 0)
                cnt = jnp.minimum(tpos + half_w, seq_len) - jnp.maximum(tpos - half_w, 0)
                return wsum[row0:row0 + POOL_EDGE] / cnt.astype(jnp.float32)

            mean = jnp.concatenate(
                [edge(0), wsum[POOL_EDGE:tile - POOL_EDGE] * inv_full, edge(tile - POOL_EDGE)], axis=0)
            pooled = mean - xs[MARGIN:MARGIN + tile]
            mix_scr[qs, ATTN_W + hf * 128:ATTN_W + (hf + 1) * 128] = pooled.astype(jnp.bfloat16)
        yield

        def glu(a, b):
            av = proj_scr[a:b, C_CONV:C_CONV + CONV_W]
            gv = proj_scr[a:b, C_CONV + CONV_W:C_CONV + 2 * CONV_W]
            return av * jax.nn.sigmoid(gv)

        fill_slab(glu)
        for blk in range(tile // CONV_ROWS):
            b0 = blk * CONV_ROWS
            y = None
            for rr in range(8):
                z = None
                for a in range((first_tap + CONV_WIDTH + 7) // 8):
                    o = 8 * a + rr
                    if first_tap <= o < first_tap + CONV_WIDTH:
                        j = o - first_tap
                        term = slab_scr[b0 + 8 * a:b0 + 8 * a + CONV_ROWS + 8, :] * r.conv_dw[0, j:j + 1, :]
                        z = term if z is None else z + term
                zr = z[rr:rr + CONV_ROWS]
                y = zr if y is None else y + zr
            y = _rms(y + r.conv_b[0], r.conv_norm[0])
            y = y * jax.nn.sigmoid(y)
            mix_scr[q * tile + b0:q * tile + b0 + CONV_ROWS, conv_cols] = y.astype(jnp.bfloat16)
            yield

    pool_out = _dot(mix_scr[:, pool_cols], r.pool_w[0]) * r.pool_scale[0]
    mix_scr[:, pool_cols] = pool_out.astype(jnp.bfloat16)
    conv_out = _dot(mix_scr[:, conv_cols], r.conv_pw[0])
    mix_scr[:, conv_cols] = conv_out.astype(jnp.bfloat16)
    yield

    gm_b = r.gm_b[0]
    for q in range(n_seq):
        pr = slice(q * rows + c0, q * rows + c1)
        gu = jax.nn.gelu(proj_scr[pr, C_GM:C_GM + GM_W])
        gv = jax.nn.gelu(proj_scr[pr, C_GM + GM_W:C_GM + 2 * GM_W])
        gv = _rms(gv, r.gm_norm[0]).astype(jnp.bfloat16)
        for c in range(tile // GM_CHUNK):
            cs = slice(c * GM_CHUNK, (c + 1) * GM_CHUNK)
            parts = []
            for g in range(GM_GROUPS):
                parts.append(_dot(r.gm_ws[0, g], gv[cs, g * GM_GROUP_W:(g + 1) * GM_GROUP_W]))
            sv = jnp.concatenate(parts, axis=-1) + gm_b
            ms = slice(q * tile + c * GM_CHUNK, q * tile + (c + 1) * GM_CHUNK)
            mix_scr[ms, ATTN_W + POOL_W + CONV_W:D_MODEL] = (gu[cs] * sv).astype(jnp.bfloat16)
        yield

    half = n_seq * tile // 2
    for hq in range(2):
        mixed = _dot(mix_scr[hq * half:(hq + 1) * half, :], r.w_out[0])
        for lo in range(0, half, TAIL_ROWS):
            row = hq * half + lo
            q, t0 = row // tile, row % tile
            x1 = r.x[q, t0:t0 + TAIL_ROWS, :] + g1 * mixed[lo:lo + TAIL_ROWS]
            r.x1_scr[slot, row:row + TAIL_ROWS, :] = x1
            r.h2_scr[slot, row:row + TAIL_ROWS, :] = (_rms(x1, norm2) * (1.0 + sc2) + sh2).astype(jnp.bfloat16)
            yield


def _mixer_pieces(cfg):
    attn = (cfg.tile // ATTN_BLOCK if cfg.latent else cfg.n_seq) * N_KV_HEADS
    return 1 + 3 * attn +cfg.n_seq * (1 + cfg.tile // CONV_ROWS) + 1 + cfg.n_seq


def _layer_kernel(*refs, cfg):
    names = []
    if cfg.halo:
        names += ['xprev', 'x', 'xnext']
    else:
        names += ['x']
    names += ['mod']
    if cfg.latent:
        names += ['modp']
    names += ['norm1', 'norm2', 'w_in', 'w_out', 'sink', 'pool_w', 'pool_scale', 'conv_dw', 'conv_b', 'conv_norm',
              'conv_pw', 'gm_norm', 'gm_ws', 'gm_b', 'w1', 'w2']
    if cfg.latent:
        names += ['cos', 'sin', 'ck', 'cv']
    if cfg.last:
        names += ['fnorm']
    if not cfg.latent:
        names += ['kacc', 'vacc']
    names += ['out']
    if not cfg.latent:
        names += ['kout', 'vout']
    names += ['h_scr', 'proj_scr', 'q_scr', 'k_scr', 'v_scr', 'slab_scr', 'mix_scr', 'acc_scr', 'x1_scr', 'h2_scr']
    assert len(names) == len(refs)
    r = types.SimpleNamespace(**dict(zip(names, refs)))
    if not cfg.latent:
        r.modp = r.mod

    crows = cfg.n_seq * cfg.tile
    step = pl.program_id(0)

    if cfg.fill_body:
        @pl.when(step == 0)
        def _():
            for _ in _mixer_half(r, cfg):
                pass
        steady = (step > 0) & (step < cfg.n_steps)
    else:
        @pl.when(step == 0)
        def _():
            r.x1_scr[1] = jnp.zeros((crows, D_MODEL), jnp.float32)
            r.h2_scr[1] = jnp.zeros((crows, D_MODEL), jnp.bfloat16)
        steady = step < cfg.n_steps

    @pl.when(step == cfg.n_steps)
    def _():
        for _ in _mlp_half(r, cfg):
            pass

    @pl.when(steady)
    def _():
        mlp = _mlp_half(r, cfg)
        mixer = _mixer_half(r, cfg)
        n_mlp = 2 * (D_FF // FF_CHUNK)
        n_mix = _mixer_pieces(cfg)
        tail_mlp = 3
        next(mlp)
        next(mixer)
        done = 1
        for i in range(n_mix):
            next(mixer)
            target = ((i + 1) * (n_mlp - tail_mlp)) // n_mix
            while done < target:
                next(mlp)
                done += 1
        n_tail = crows // TAIL_ROWS
        for k, _ in enumerate(mixer):
            target = n_mlp - tail_mlp + ((k + 1) * tail_mlp) // n_tail
            while done < min(target, n_mlp):
                next(mlp)
                done += 1
        for _ in mlp:
            pass


def _const_spec(shape, index):
    return pl.BlockSpec(shape, lambda s: index, pipeline_mode=pl.Buffered(1))


def _layer_call(x, layer, mod_all, params, *, tile, latent, last, rope=None, cache=None, final_norm=None,
                kv_acc=None):
    batch, seq_len, _ = x.shape
    halo = HALO if latent else 0
    rows = tile + 2 * halo
    n_tiles = seq_len // tile
    n_seq = STEP_ROWS // tile
    assert seq_len % tile == 0 and tile % ATTN_BLOCK == 0 and STEP_ROWS % tile == 0
    if latent:
        assert n_seq == 1
    else:
        assert n_tiles == 1 and batch % n_seq == 0
    n_steps = (batch // n_seq) * n_tiles
    last_step = n_steps - 1

    def cur(s):
        return jnp.minimum(s, last_step)

    def prev(s):
        return jnp.maximum(s - 1, 0)

    in_specs = []
    args = []
    if halo:
        hb = tile // halo
        n_hblk = seq_len // halo
        in_specs.append(pl.BlockSpec(
            (1, halo, D_MODEL),
            lambda s: (cur(s) // n_tiles, jnp.maximum((cur(s) % n_tiles) * hb - 1, 0), 0)))
        args.append(x)
    in_specs.append(pl.BlockSpec((n_seq, tile, D_MODEL), lambda s: (cur(s) // n_tiles, cur(s) % n_tiles, 0)))
    args.append(x)
    if halo:
        in_specs.append(pl.BlockSpec(
            (1, halo, D_MODEL),
            lambda s: (cur(s) // n_tiles, jnp.minimum((cur(s) % n_tiles + 1) * hb, n_hblk - 1), 0)))
        args.append(x)

    if latent:
        in_specs.append(pl.BlockSpec((1, 1, N_MOD * D_MODEL),
                                     lambda s: (layer * MOD_ROWS + cur(s) // n_tiles, 0, 0)))
        args.append(mod_all)
        in_specs.append(pl.BlockSpec((1, 1, N_MOD * D_MODEL),
                                     lambda s: (layer * MOD_ROWS + prev(s) // n_tiles, 0, 0)))
        args.append(mod_all)
    else:
        in_specs.append(pl.BlockSpec((1, 1, N_MOD * D_MODEL), lambda s: (layer * MOD_ROWS + CTX_MOD_ROW, 0, 0)))
        args.append(mod_all)

    def add_const(arr, block):
        idx = (layer,) + (0,) * (len(block) - 1)
        in_specs.append(_const_spec(block, idx))
        args.append(arr)

    add_const(params['norm1'], (1, 1, D_MODEL))
    add_const(params['norm2'], (1, 1, D_MODEL))
    add_const(params['w_in'], (1, D_MODEL, IN_W))
    add_const(params['w_out'], (1, D_MODEL, D_MODEL))
    in_specs.append(pl.BlockSpec(memory_space=pltpu.SMEM))
    args.append(params['attn_sink'])
    add_const(params['pool_w'], (1, POOL_W, POOL_W))
    add_const(params['pool_scale'], (1, 1, POOL_W))
    add_const(params['conv_dw'], (1, CONV_WIDTH, CONV_W))
    add_const(params['conv_b'], (1, 1, CONV_W))
    add_const(params['conv_norm'], (1, 1, CONV_W))
    add_const(params['conv_pw'], (1, CONV_W, CONV_W))
    add_const(params['gm_norm'], (1, 1, GM_W))
    add_const(params['gm_ws'], (1, GM_GROUPS, GM_CHUNK, GM_CHUNK))
    add_const(params['gm_b'], (1, GM_CHUNK, GM_W))
    add_const(params['w_mlp1'], (1, D_MODEL, D_FF))
    add_const(params['w_mlp2'], (1, D_FF, D_MODEL))
    if latent:
        cos_t, sin_t = rope
        in_specs.append(_const_spec(cos_t.shape, (0, 0)))
        args.append(cos_t)
        in_specs.append(_const_spec(sin_t.shape, (0, 0)))
        args.append(sin_t)
        ck, cv = cache
        past = ck.shape[2]
        in_specs.append(pl.BlockSpec((1, 1, past, KV_W), lambda s: (cur(s) // n_tiles, layer, 0, 0)))
        args.append(ck)
        in_specs.append(pl.BlockSpec((1, 1, past, KV_W), lambda s: (cur(s) // n_tiles, layer, 0, 0)))
        args.append(cv)
    if last:
        in_specs.append(_const_spec((1, D_MODEL), (0, 0)))
        args.append(final_norm)

    out_shape = [jax.ShapeDtypeStruct(x.shape, jnp.float32)]
    out_specs = [pl.BlockSpec((n_seq, tile, D_MODEL), lambda s: (prev(s) // n_tiles, prev(s) % n_tiles, 0))]
    aliases = {}
    if not latent:
        for i, acc in enumerate(kv_acc):
            in_specs.append(pl.BlockSpec(memory_space=pl.ANY))
            args.append(acc)
            aliases[len(args) - 1] = 1 + i
            out_shape.append(jax.ShapeDtypeStruct(acc.shape, jnp.float32))
            out_specs.append(pl.BlockSpec((n_seq, 1, tile, KV_W), lambda s: (cur(s), layer, 0, 0)))

    crows = n_seq * tile
    scratch = [
        pltpu.VMEM((n_seq * rows, D_MODEL), jnp.bfloat16),
        pltpu.VMEM((n_seq * rows, IN_W), jnp.float32),
        pltpu.VMEM((N_Q_HEADS, crows, HEAD_DIM), jnp.bfloat16),
        pltpu.VMEM((N_KV_HEADS, n_seq * rows, HEAD_DIM), jnp.bfloat16),
        pltpu.VMEM((N_KV_HEADS, n_seq * rows, HEAD_DIM), jnp.bfloat16),
        pltpu.VMEM((tile + 2 * MARGIN, 256), jnp.float32),
        pltpu.VMEM((crows, D_MODEL), jnp.bfloat16),
        pltpu.VMEM((crows, D_MODEL), jnp.float32),
        pltpu.VMEM((2, crows, D_MODEL), jnp.float32),
        pltpu.VMEM((2, crows, D_MODEL), jnp.bfloat16),
    ]
    cfg = types.SimpleNamespace(layer=layer, tile=tile, n_seq=n_seq, halo=halo, seq_len=seq_len, n_steps=n_steps,
                                latent=latent, last=last, fill_body=not latent)
    kern = functools.partial(_layer_kernel, cfg=cfg)
    outs = pl.pallas_call(
        kern,
        grid=(n_steps + 1,),
        in_specs=in_specs,
        out_specs=out_specs,
        out_shape=out_shape,
        input_output_aliases=aliases,
        scratch_shapes=scratch,
        compiler_params=pltpu.CompilerParams(
            dimension_semantics=("arbitrary",),
            vmem_limit_bytes=VMEM_LIMIT_BYTES),
        name=("latent" if latent else "context") + "_layer",
    )(*args)
    return outs


def _rope_tables(seq_len):
    rows = seq_len // GRID_W
    row = np.repeat(np.arange(rows), GRID_W).astype(np.float32)
    col = np.tile(np.arange(GRID_W), rows).astype(np.float32)
    inv = (np.float32(ROPE_BASE) ** (-np.arange(ROPE_PAIRS, dtype=np.float32) / np.float32(ROPE_PAIRS))).astype(
        np.float32)
    ang = np.concatenate([row[:, None] * inv, col[:, None] * inv], axis=-1)
    cos, sin = np.cos(ang.astype(np.float64)), np.sin(ang.astype(np.float64))
    cos_t = np.tile(np.concatenate([cos, cos], axis=-1), (1, 2)).astype(np.float32)
    sin_t = np.tile(np.concatenate([-sin, sin], axis=-1), (1, 2)).astype(np.float32)
    pad = ((HALO, HALO), (0, 0))
    return jnp.asarray(np.pad(cos_t, pad)), jnp.asarray(np.pad(sin_t, pad))


def _prep_params(norm1, norm2, w_in, w_out, attn_sink, pool_w, pool_scale, conv_dw, conv_b, conv_norm,
                 conv_pw, gm_norm, gm_ws, gm_b, w_mlp1, w_mlp2):
    depth = w_in.shape[0]
    bf = jnp.bfloat16
    gw = POOL_W // POOL_GROUPS
    pool_bd = jnp.zeros((depth, POOL_W, POOL_W), jnp.float32)
    for g in range(POOL_GROUPS):
        pool_bd = pool_bd.at[:, g * gw:(g + 1) * gw, g * gw:(g + 1) * gw].set(pool_w[:, g])
    gm_b_full = jnp.repeat(jnp.transpose(gm_b, (0, 2, 1)), GM_GROUP_W, axis=-1)
    return {
        'norm1': norm1.reshape(depth, 1, D_MODEL),
        'norm2': norm2.reshape(depth, 1, D_MODEL),
        'w_in': w_in.astype(bf),
        'w_out': w_out.astype(bf),
        'attn_sink': attn_sink,
        'pool_w': pool_bd.astype(bf),
        'pool_scale': pool_scale.reshape(depth, 1, POOL_W),
        'conv_dw': conv_dw,
        'conv_b': conv_b.reshape(depth, 1, CONV_W),
        'conv_norm': conv_norm.reshape(depth, 1, CONV_W),
        'conv_pw': conv_pw.astype(bf),
        'gm_norm': gm_norm.reshape(depth, 1, GM_W),
        'gm_ws': gm_ws.astype(bf),
        'gm_b': gm_b_full,
        'w_mlp1': w_mlp1.astype(bf),
        'w_mlp2': w_mlp2.astype(bf),
    }


def kernel(x_prompt, x_sample, cache_k, cache_v, c, c_ctx, w_ada, b_ada, norm1, norm2, w_in, w_out, attn_sink,
           pool_w, pool_scale, conv_dw, conv_b, conv_norm, conv_pw, gm_norm, gm_ws, gm_b, w_mlp1, w_mlp2,
           final_norm):
    depth = w_in.shape[0]
    dec_batch, dec_seq, _ = x_sample.shape
    batch, seq, _ = x_prompt.shape
    past = cache_k.shape[2]

    cond = jnp.zeros((MOD_ROWS, D_MODEL), jnp.float32)
    cond = cond.at[:dec_batch].set(c).at[CTX_MOD_ROW].set(c_ctx)
    mod_all = _ada_mod(cond, w_ada, b_ada).reshape(depth * MOD_ROWS, 1, N_MOD * D_MODEL)

    params = _prep_params(norm1, norm2, w_in, w_out, attn_sink, pool_w, pool_scale, conv_dw, conv_b,
                          conv_norm, conv_pw, gm_norm, gm_ws, gm_b, w_mlp1, w_mlp2)
    rope = _rope_tables(dec_seq)
    ck = cache_k.reshape(dec_batch, depth, past, KV_W)
    cv = cache_v.reshape(dec_batch, depth, past, KV_W)
    fnorm = final_norm.reshape(1, D_MODEL)

    xc, xs = x_prompt, x_sample
    new_k = jnp.zeros((batch, depth, seq, KV_W), jnp.float32)
    new_v = jnp.zeros((batch, depth, seq, KV_W), jnp.float32)
    for l in range(depth):
        last = l == depth - 1
        xc, new_k, new_v = _layer_call(xc, l, mod_all, params, tile=seq, latent=False, last=last,
                                       final_norm=fnorm if last else None, kv_acc=(new_k, new_v))
        (xs,) = _layer_call(xs, l, mod_all, params, tile=STEP_ROWS, latent=True, last=last,
                            rope=rope, cache=(ck, cv), final_norm=fnorm if last else None)
    new_k = new_k.reshape(batch, depth, seq, N_KV_HEADS, HEAD_DIM)
    new_v = new_v.reshape(batch, depth, seq, N_KV_HEADS, HEAD_DIM)
    return (xc, xs, new_k, new_v)
```

```python
import functools
import types

import jax
import jax.numpy as jnp
import numpy as np
from jax import lax
from jax.experimental import pallas as pl
from jax.experimental.pallas import tpu as pltpu

D_MODEL = 1024
DEPTH = 4
ATTN_W = 256
POOL_W = 256
CONV_W = 256
GM_W = 256
HEAD_DIM = 64
N_Q_HEADS = 4
N_KV_HEADS = 2
Q_PER_KV = 2
KV_W = 128
ATTN_BLOCK = 128
GRID_W = 64
ROPE_BASE = 10000.0
ROPE_PAIRS = 16
POOL_GROUPS = 4
POOL_SIZES = (2, 4, 8, 16)
CONV_WIDTH = 31
CONV_HALF = 15
GM_GROUPS = 4
GM_GROUP_W = 64
GM_CHUNK = 128
D_FF = 4096
N_MOD = 6
EPS = 1e-6
NEG_INF = -1e30
LOG2_E = 1.4426950408889634
IN_W = 1792
C_Q, C_K, C_V, C_POOL, C_CONV, C_GM = 0, 256, 384, 512, 768, 1280

HALO = 128
MARGIN = 16
POOL_EDGE = 16
MOD_ROWS = 16
CTX_MOD_ROW = 8
FF_CHUNK = 512
CONV_ROWS = 64
TAIL_ROWS = 128
STEP_ROWS = 512
VMEM_LIMIT_BYTES = 58 * 1024 * 1024


def _rms(x, g):
    ms = jnp.mean(x * x, axis=-1, keepdims=True)
    return x * lax.rsqrt(ms + EPS) * g


def _dot(a, b):
    return jnp.dot(a, b, preferred_element_type=jnp.float32)


def _dot_nt(a, b):
    return lax.dot_general(a, b, (((1,), (1,)), ((), ())), preferred_element_type=jnp.float32)


def _ada_kernel(cond_ref, w_ref, b_ref, o_ref):
    s = cond_ref[...]
    s = s * jax.nn.sigmoid(s)
    o_ref[0] = _dot(s.astype(jnp.bfloat16), w_ref[0].astype(jnp.bfloat16)) + b_ref[0]


def _ada_mod(cond, w_ada, b_ada):
    tn = 1536
    nt = (N_MOD * D_MODEL) // tn
    return pl.pallas_call(
        _ada_kernel,
        grid=(DEPTH, nt),
        in_specs=[
            pl.BlockSpec((MOD_ROWS, D_MODEL), lambda l, j: (0, 0)),
            pl.BlockSpec((1, D_MODEL, tn), lambda l, j: (l, 0, j)),
            pl.BlockSpec((1, 1, tn), lambda l, j: (l, 0, j)),
        ],
        out_specs=pl.BlockSpec((1, MOD_ROWS, tn), lambda l, j: (l, 0, j)),
        out_shape=jax.ShapeDtypeStruct((DEPTH, MOD_ROWS, N_MOD * D_MODEL), jnp.float32),
        compiler_params=pltpu.CompilerParams(
            dimension_semantics=("arbitrary", "arbitrary"),
            vmem_limit_bytes=VMEM_LIMIT_BYTES),
        name="ada_mod",
    )(cond, w_ada, b_ada.reshape(DEPTH, 1, N_MOD * D_MODEL))


def _mlp_half(r, cfg):
    tile, n_seq = cfg.tile, cfg.n_seq
    slot = 1 - pl.program_id(0) % 2
    g2 = r.modp[0][:, 5 * D_MODEL:6 * D_MODEL]
    n_chunks = D_FF // FF_CHUNK
    for c in range(n_chunks):
        a = _dot(r.h2_scr[slot], r.w1[0, :, c * FF_CHUNK:(c + 1) * FF_CHUNK])
        a = jnp.square(jnp.maximum(a, 0.0)).astype(jnp.bfloat16)
        yield
        w2_c = r.w2[0, c * FF_CHUNK:(c + 1) * FF_CHUNK, :]
        if c < n_chunks - 1:
            f = _dot(a, w2_c)
            if c == 0:
                r.acc_scr[...] = f
            else:
                r.acc_scr[...] += f
            yield
    x2 = r.x1_scr[slot] + g2 * (r.acc_scr[...] + _dot(a, w2_c))
    yield
    if cfg.last:
        x2 = _rms(x2, r.fnorm[...])
    for q in range(n_seq):
        r.out[q] = x2[q * tile:(q + 1) * tile]


def _mixer_half(r, cfg):
    tile, n_seq, halo, seq_len, latent, layer = cfg.tile, cfg.n_seq, cfg.halo, cfg.seq_len, cfg.latent, cfg.layer
    rows = tile + 2 * halo
    step = pl.program_id(0)
    slot = step % 2
    cur = jnp.minimum(step, cfg.n_steps - 1)
    n_tiles = seq_len // tile
    ti = cur % n_tiles
    is_first = ti == 0
    is_last_tile = ti == n_tiles - 1
    h_scr, proj_scr, q_scr, k_scr, v_scr, slab_scr, mix_scr = (
        r.h_scr, r.proj_scr, r.q_scr, r.k_scr, r.v_scr, r.slab_scr, r.mix_scr)

    mod = r.mod[0]
    sh1 = mod[:, 0 * D_MODEL:1 * D_MODEL]
    sc1 = mod[:, 1 * D_MODEL:2 * D_MODEL]
    g1 = mod[:, 2 * D_MODEL:3 * D_MODEL]
    sh2 = mod[:, 3 * D_MODEL:4 * D_MODEL]
    sc2 = mod[:, 4 * D_MODEL:5 * D_MODEL]
    norm1 = r.norm1[0]
    norm2 = r.norm2[0]

    def norm_mod1(x):
        return (_rms(x, norm1) * (1.0 + sc1) + sh1).astype(jnp.bfloat16)

    c0 = halo
    c1 = halo + tile
    if halo:
        h_scr[0:halo, :] = norm_mod1(r.xprev[0])
        h_scr[c1:rows, :] = norm_mod1(r.xnext[0])
        h_scr[c0:c1, :] = norm_mod1(r.x[0])
        m0, m1 = c0 - MARGIN, c1 + MARGIN
        proj_scr[c0:c1, C_Q:C_K] = _dot(h_scr[c0:c1, :], r.w_in[0, :, C_Q:C_K])
        proj_scr[:, C_K:C_POOL] = _dot(h_scr[...], r.w_in[0, :, C_K:C_POOL])
        proj_scr[m0:m1, C_POOL:C_GM] = _dot(h_scr[m0:m1, :], r.w_in[0, :, C_POOL:C_GM])
        proj_scr[c0:c1, C_GM:IN_W] = _dot(h_scr[c0:c1, :], r.w_in[0, :, C_GM:IN_W])
    else:
        for q in range(n_seq):
            h_scr[q * tile:(q + 1) * tile, :] = norm_mod1(r.x[q])
        proj_scr[...] = _dot(h_scr[...], r.w_in[0])
    yield

    scale = HEAD_DIM ** -0.5 * LOG2_E
    if latent:
        row0 = ti * tile
        cos_t = r.cos[pl.ds(pl.multiple_of(row0, tile), rows), :]
        sin_t = r.sin[pl.ds(pl.multiple_of(row0, tile), rows), :]
        lane = lax.broadcasted_iota(jnp.int32, (1, 128), 1)
        first_half = (lane % HEAD_DIM) < (HEAD_DIM // 2)

        def rope(z, cs, sn):
            swapped = jnp.where(first_half, pltpu.roll(z, 96, 1), pltpu.roll(z, 32, 1))
            return z * cs + swapped * sn

    k_all = proj_scr[:, C_K:C_K + KV_W]
    if not latent:
        for q in range(n_seq):
            r.kout[q, 0] = k_all[q * tile:(q + 1) * tile]
            r.vout[q, 0] = proj_scr[q * tile:(q + 1) * tile, C_V:C_V + KV_W]
    if latent:
        k_all = rope(k_all, cos_t, sin_t)
    k_bf = k_all.astype(jnp.bfloat16)
    v_bf = proj_scr[:, C_V:C_V + KV_W].astype(jnp.bfloat16)
    for g in range(N_KV_HEADS):
        k_scr[g] = k_bf[:, g * HEAD_DIM:(g + 1) * HEAD_DIM]
        v_scr[g] = v_bf[:, g * HEAD_DIM:(g + 1) * HEAD_DIM]
    for q in range(n_seq):
        for half in range(2):
            qh = proj_scr[q * rows + c0:q * rows + c1, C_Q + half * 128:C_Q + (half + 1) * 128]
            if latent:
                qh = rope(qh, cos_t[c0:c1], sin_t[c0:c1])
            qh = (qh * scale).astype(jnp.bfloat16)
            for s in range(2):
                q_scr[2 * half + s, q * tile:(q + 1) * tile, :] = qh[:, s * HEAD_DIM:(s + 1) * HEAD_DIM]
    yield

    if latent:
        ck_bf = r.ck[0, 0].astype(jnp.bfloat16)
        cv_bf = r.cv[0, 0].astype(jnp.bfloat16)
        r_i = lax.broadcasted_iota(jnp.int32, (2 * ATTN_BLOCK, 3 * ATTN_BLOCK), 0) % ATTN_BLOCK
        c_i = lax.broadcasted_iota(jnp.int32, (2 * ATTN_BLOCK, 3 * ATTN_BLOCK), 1)
        band_ok = jnp.abs(r_i - (c_i - ATTN_BLOCK)) <= HALO
        n_qblk = tile // ATTN_BLOCK
        head_row = lax.broadcasted_iota(jnp.int32, (2 * ATTN_BLOCK, 1), 0) < ATTN_BLOCK
        for j in range(n_qblk):
            bias = jnp.where(band_ok, 0.0, NEG_INF)
            if j == 0:
                lo = jnp.where(is_first, ATTN_BLOCK, 0)
                bias = jnp.where(c_i >= lo, bias, NEG_INF)
            if j == n_qblk - 1:
                hi = jnp.where(is_last_tile, 2 * ATTN_BLOCK, 3 * ATTN_BLOCK)
                bias = jnp.where(c_i < hi, bias, NEG_INF)
            js = slice(j * ATTN_BLOCK, (j + 1) * ATTN_BLOCK)
            ks = slice(j * ATTN_BLOCK, (j + 3) * ATTN_BLOCK)
            for g in range(N_KV_HEADS):
                q2 = jnp.concatenate([q_scr[2 * g, js, :], q_scr[2 * g + 1, js, :]], axis=0)
                s_loc = _dot_nt(q2, k_scr[g, ks, :]) + bias
                s_ctx = _dot_nt(q2, ck_bf[:, g * HEAD_DIM:(g + 1) * HEAD_DIM])
                sink = jnp.where(head_row, r.sink[layer, 2 * g] * LOG2_E, r.sink[layer, 2 * g + 1] * LOG2_E)
                m = jnp.maximum(jnp.maximum(jnp.max(s_loc, axis=-1, keepdims=True),
                                            jnp.max(s_ctx, axis=-1, keepdims=True)), sink)
                yield
                p_loc = jnp.exp2(s_loc - m)
                p_ctx = jnp.exp2(s_ctx - m)
                den = (jnp.sum(p_loc, axis=-1, keepdims=True) + jnp.sum(p_ctx, axis=-1, keepdims=True)
                       + jnp.exp2(sink - m))
                yield
                o = (_dot(p_loc.astype(jnp.bfloat16), v_scr[g, ks, :])
                     + _dot(p_ctx.astype(jnp.bfloat16), cv_bf[:, g * HEAD_DIM:(g + 1) * HEAD_DIM]))
                o = (o / den).astype(jnp.bfloat16)
                mix_scr[js, (2 * g) * HEAD_DIM:(2 * g + 1) * HEAD_DIM] = o[:ATTN_BLOCK]
                mix_scr[js, (2 * g + 1) * HEAD_DIM:(2 * g + 2) * HEAD_DIM] = o[ATTN_BLOCK:]
                yield
    else:
        head_row = lax.broadcasted_iota(jnp.int32, (2 * tile, 1), 0) < tile
        for q in range(n_seq):
            qs = slice(q * tile, (q + 1) * tile)
            for g in range(N_KV_HEADS):
                q2 = jnp.concatenate([q_scr[2 * g, qs, :], q_scr[2 * g + 1, qs, :]], axis=0)
                s = _dot_nt(q2, k_scr[g, qs, :])
                sink = jnp.where(head_row, r.sink[layer, 2 * g] * LOG2_E, r.sink[layer, 2 * g + 1] * LOG2_E)
                m = jnp.maximum(jnp.max(s, axis=-1, keepdims=True), sink)
                yield
                p = jnp.exp2(s - m)
                den = jnp.sum(p, axis=-1, keepdims=True) + jnp.exp2(sink - m)
                yield
                o = (_dot(p.astype(jnp.bfloat16), v_scr[g, qs, :]) / den).astype(jnp.bfloat16)
                mix_scr[qs, (2 * g) * HEAD_DIM:(2 * g + 1) * HEAD_DIM] = o[:tile]
                mix_scr[qs, (2 * g + 1) * HEAD_DIM:(2 * g + 2) * HEAD_DIM] = o[tile:]
                yield

    srows = tile + 2 * MARGIN
    lo_grp = lax.broadcasted_iota(jnp.int32, (1, 128), 1) < POOL_W // POOL_GROUPS
    pool_cols = slice(ATTN_W, ATTN_W + POOL_W)
    conv_cols = slice(ATTN_W + POOL_W, ATTN_W + POOL_W + CONV_W)
    first_tap = MARGIN - CONV_HALF

    for q in range(n_seq):
        base = q * rows
        qs = slice(q * tile, (q + 1) * tile)

        def fill_slab(fn):
            slab_scr[MARGIN:MARGIN + tile, :] = fn(base + c0, base + c1)
            zeros = jnp.zeros((MARGIN, 256), jnp.float32)
            if halo:
                left = fn(c0 - MARGIN, c0)
                right = fn(c1, c1 + MARGIN)
                slab_scr[0:MARGIN, :] = jnp.where(is_first, zeros, left)
                slab_scr[MARGIN + tile:srows, :] = jnp.where(is_last_tile, zeros, right)
            else:
                slab_scr[0:MARGIN, :] = zeros
                slab_scr[MARGIN + tile:srows, :] = zeros

        fill_slab(lambda a, b: proj_scr[a:b, C_POOL:C_POOL + POOL_W])
        for hf in range(2):
            xs = slab_scr[:, hf * 128:(hf + 1) * 128]
            size_lo, size_hi = POOL_SIZES[2 * hf], POOL_SIZES[2 * hf + 1]
            wins = {}
            acc = xs
            shift_by = 1
            for size in POOL_SIZES:
                if size > size_hi:
                    break
                acc = acc + pltpu.roll(acc, shift_by, 0)
                shift_by = size
                if size in (size_lo, size_hi):
                    shift = size // 2 - 1
                    win = acc if shift == 0 else pltpu.roll(acc, srows - shift, 0)
                    wins[size] = win[MARGIN:MARGIN + tile]
            wsum = jnp.where(lo_grp, wins[size_lo], wins[size_hi])
            half_w = jnp.where(lo_grp, size_lo // 2, size_hi // 2)
            inv_full = jnp.where(lo_grp, 1.0 / size_lo, 1.0 / size_hi)

            def edge(row0):
                tpos = ti * tile + row0 + lax.broadcasted_iota(jnp.int32, (POOL_EDGE, 1), 0)
                cnt = jnp.minimum(tpos + half_w, seq_len) - jnp.maximum(tpos - half_w, 0)
                return wsum[row0:row0 + POOL_EDGE] / cnt.astype(jnp.float32)

            mean = jnp.concatenate(
                [edge(0), wsum[POOL_EDGE:tile - POOL_EDGE] * inv_full, edge(tile - POOL_EDGE)], axis=0)
            pooled = mean - xs[MARGIN:MARGIN + tile]
            mix_scr[qs, ATTN_W + hf * 128:ATTN_W + (hf + 1) * 128] = pooled.astype(jnp.bfloat16)
        yield

        def glu(a, b):
            av = proj_scr[a:b, C_CONV:C_CONV + CONV_W]
            gv = proj_scr[a:b, C_CONV + CONV_W:C_CONV + 2 * CONV_W]
            return av * jax.nn.sigmoid(gv)

        fill_slab(glu)
        for blk in range(tile // CONV_ROWS):
            b0 = blk * CONV_ROWS
            y = None
            for rr in range(8):
                z = None
                for a in range((first_tap + CONV_WIDTH + 7) // 8):
                    o = 8 * a + rr
                    if first_tap <= o < first_tap + CONV_WIDTH:
                        j = o - first_tap
                        term = slab_scr[b0 + 8 * a:b0 + 8 * a + CONV_ROWS + 8, :] * r.conv_dw[0, j:j + 1, :]
                        z = term if z is None else z + term
                zr = z[rr:rr + CONV_ROWS]
                y = zr if y is None else y + zr
            y = _rms(y + r.conv_b[0], r.conv_norm[0])
            y = y * jax.nn.sigmoid(y)
            mix_scr[q * tile + b0:q * tile + b0 + CONV_ROWS, conv_cols] = y.astype(jnp.bfloat16)
            yield

    pool_out = _dot(mix_scr[:, pool_cols], r.pool_w[0]) * r.pool_scale[0]
    mix_scr[:, pool_cols] = pool_out.astype(jnp.bfloat16)
    conv_out = _dot(mix_scr[:, conv_cols], r.conv_pw[0])
    mix_scr[:, conv_cols] = conv_out.astype(jnp.bfloat16)
    yield

    gm_b = r.gm_b[0]
    for q in range(n_seq):
        pr = slice(q * rows + c0, q * rows + c1)
        gu = jax.nn.gelu(proj_scr[pr, C_GM:C_GM + GM_W])
        gv = jax.nn.gelu(proj_scr[pr, C_GM + GM_W:C_GM + 2 * GM_W])
        gv = _rms(gv, r.gm_norm[0]).astype(jnp.bfloat16)
        for c in range(tile // GM_CHUNK):
            cs = slice(c * GM_CHUNK, (c + 1) * GM_CHUNK)
            parts = []
            for g in range(GM_GROUPS):
                parts.append(_dot(r.gm_ws[0, g], gv[cs, g * GM_GROUP_W:(g + 1) * GM_GROUP_W]))
            sv = jnp.concatenate(parts, axis=-1) + gm_b
            ms = slice(q * tile + c * GM_CHUNK, q * tile + (c + 1) * GM_CHUNK)
            mix_scr[ms, ATTN_W + POOL_W + CONV_W:D_MODEL] = (gu[cs] * sv).astype(jnp.bfloat16)
        yield

    mixed = _dot(mix_scr[...], r.w_out[0])
    for row in range(0, n_seq * tile, TAIL_ROWS):
        if True:
            q, t0 = row // tile, row % tile
            x1 = r.x[q, t0:t0 + TAIL_ROWS, :] + g1 * mixed[row:row + TAIL_ROWS]
            r.x1_scr[slot, row:row + TAIL_ROWS, :] = x1
            r.h2_scr[slot, row:row + TAIL_ROWS, :] = (_rms(x1, norm2) * (1.0 + sc2) + sh2).astype(jnp.bfloat16)
            yield


def _mixer_pieces(cfg):
    attn = (cfg.tile // ATTN_BLOCK if cfg.latent else cfg.n_seq) * N_KV_HEADS
    return 1 + 3 * attn +cfg.n_seq * (1 + cfg.tile // CONV_ROWS) + 1 + cfg.n_seq


def _layer_kernel(*refs, cfg):
    names = []
    if cfg.halo:
        names += ['xprev', 'x', 'xnext']
    else:
        names += ['x']
    names += ['mod']
    if cfg.latent:
        names += ['modp']
    names += ['norm1', 'norm2', 'w_in', 'w_out', 'sink', 'pool_w', 'pool_scale', 'conv_dw', 'conv_b', 'conv_norm',
              'conv_pw', 'gm_norm', 'gm_ws', 'gm_b', 'w1', 'w2']
    if cfg.latent:
        names += ['cos', 'sin', 'ck', 'cv']
    if cfg.last:
        names += ['fnorm']
    if not cfg.latent:
        names += ['kacc', 'vacc']
    names += ['out']
    if not cfg.latent:
        names += ['kout', 'vout']
    names += ['h_scr', 'proj_scr', 'q_scr', 'k_scr', 'v_scr', 'slab_scr', 'mix_scr', 'acc_scr', 'x1_scr', 'h2_scr']
    assert len(names) == len(refs)
    r = types.SimpleNamespace(**dict(zip(names, refs)))
    if not cfg.latent:
        r.modp = r.mod

    crows = cfg.n_seq * cfg.tile
    step = pl.program_id(0)

    @pl.when(step == 0)
    def _():
        r.x1_scr[1] = jnp.zeros((crows, D_MODEL), jnp.float32)
        r.h2_scr[1] = jnp.zeros((crows, D_MODEL), jnp.bfloat16)

    @pl.when(step == cfg.n_steps)
    def _():
        for _ in _mlp_half(r, cfg):
            pass

    @pl.when(step < cfg.n_steps)
    def _():
        mlp = _mlp_half(r, cfg)
        mixer = _mixer_half(r, cfg)
        n_mlp = 2 * (D_FF // FF_CHUNK)
        n_mix = _mixer_pieces(cfg)
        tail_mlp = 3
        next(mlp)
        next(mixer)
        done = 1
        for i in range(n_mix):
            next(mixer)
            target = ((i + 1) * (n_mlp - tail_mlp)) // n_mix
            while done < target:
                next(mlp)
                done += 1
        n_tail = crows // TAIL_ROWS
        for k, _ in enumerate(mixer):
            target = n_mlp - tail_mlp + ((k + 1) * tail_mlp) // n_tail
            while done < min(target, n_mlp):
                next(mlp)
                done += 1
        for _ in mlp:
            pass


def _const_spec(shape, index):
    return pl.BlockSpec(shape, lambda s: index, pipeline_mode=pl.Buffered(1))


def _layer_call(x, layer, mod_all, params, *, tile, latent, last, rope=None, cache=None, final_norm=None,
                kv_acc=None):
    batch, seq_len, _ = x.shape
    halo = HALO if latent else 0
    rows = tile + 2 * halo
    n_tiles = seq_len // tile
    n_seq = STEP_ROWS // tile
    assert seq_len % tile == 0 and tile % ATTN_BLOCK == 0 and STEP_ROWS % tile == 0
    if latent:
        assert n_seq == 1
    else:
        assert n_tiles == 1 and batch % n_seq == 0
    n_steps = (batch // n_seq) * n_tiles
    last_step = n_steps - 1

    def cur(s):
        return jnp.minimum(s, last_step)

    def prev(s):
        return jnp.maximum(s - 1, 0)

    in_specs = []
    args = []
    if halo:
        hb = tile // halo
        n_hblk = seq_len // halo
        in_specs.append(pl.BlockSpec(
            (1, halo, D_MODEL),
            lambda s: (cur(s) // n_tiles, jnp.maximum((cur(s) % n_tiles) * hb - 1, 0), 0)))
        args.append(x)
    in_specs.append(pl.BlockSpec((n_seq, tile, D_MODEL), lambda s: (cur(s) // n_tiles, cur(s) % n_tiles, 0)))
    args.append(x)
    if halo:
        in_specs.append(pl.BlockSpec(
            (1, halo, D_MODEL),
            lambda s: (cur(s) // n_tiles, jnp.minimum((cur(s) % n_tiles + 1) * hb, n_hblk - 1), 0)))
        args.append(x)

    if latent:
        in_specs.append(pl.BlockSpec((1, 1, N_MOD * D_MODEL),
                                     lambda s: (layer * MOD_ROWS + cur(s) // n_tiles, 0, 0)))
        args.append(mod_all)
        in_specs.append(pl.BlockSpec((1, 1, N_MOD * D_MODEL),
                                     lambda s: (layer * MOD_ROWS + prev(s) // n_tiles, 0, 0)))
        args.append(mod_all)
    else:
        in_specs.append(pl.BlockSpec((1, 1, N_MOD * D_MODEL), lambda s: (layer * MOD_ROWS + CTX_MOD_ROW, 0, 0)))
        args.append(mod_all)

    def add_const(arr, block):
        idx = (layer,) + (0,) * (len(block) - 1)
        in_specs.append(_const_spec(block, idx))
        args.append(arr)

    add_const(params['norm1'], (1, 1, D_MODEL))
    add_const(params['norm2'], (1, 1, D_MODEL))
    add_const(params['w_in'], (1, D_MODEL, IN_W))
    add_const(params['w_out'], (1, D_MODEL, D_MODEL))
    in_specs.append(pl.BlockSpec(memory_space=pltpu.SMEM))
    args.append(params['attn_sink'])
    add_const(params['pool_w'], (1, POOL_W, POOL_W))
    add_const(params['pool_scale'], (1, 1, POOL_W))
    add_const(params['conv_dw'], (1, CONV_WIDTH, CONV_W))
    add_const(params['conv_b'], (1, 1, CONV_W))
    add_const(params['conv_norm'], (1, 1, CONV_W))
    add_const(params['conv_pw'], (1, CONV_W, CONV_W))
    add_const(params['gm_norm'], (1, 1, GM_W))
    add_const(params['gm_ws'], (1, GM_GROUPS, GM_CHUNK, GM_CHUNK))
    add_const(params['gm_b'], (1, GM_CHUNK, GM_W))
    add_const(params['w_mlp1'], (1, D_MODEL, D_FF))
    add_const(params['w_mlp2'], (1, D_FF, D_MODEL))
    if latent:
        cos_t, sin_t = rope
        in_specs.append(_const_spec(cos_t.shape, (0, 0)))
        args.append(cos_t)
        in_specs.append(_const_spec(sin_t.shape, (0, 0)))
        args.append(sin_t)
        ck, cv = cache
        past = ck.shape[2]
        in_specs.append(pl.BlockSpec((1, 1, past, KV_W), lambda s: (cur(s) // n_tiles, layer, 0, 0)))
        args.append(ck)
        in_specs.append(pl.BlockSpec((1, 1, past, KV_W), lambda s: (cur(s) // n_tiles, layer, 0, 0)))
        args.append(cv)
    if last:
        in_specs.append(_const_spec((1, D_MODEL), (0, 0)))
        args.append(final_norm)

    out_shape = [jax.ShapeDtypeStruct(x.shape, jnp.float32)]
    out_specs = [pl.BlockSpec((n_seq, tile, D_MODEL), lambda s: (prev(s) // n_tiles, prev(s) % n_tiles, 0))]
    aliases = {}
    if not latent:
        for i, acc in enumerate(kv_acc):
            in_specs.append(pl.BlockSpec(memory_space=pl.ANY))
            args.append(acc)
            aliases[len(args) - 1] = 1 + i
            out_shape.append(jax.ShapeDtypeStruct(acc.shape, jnp.float32))
            out_specs.append(pl.BlockSpec((n_seq, 1, tile, KV_W), lambda s: (cur(s), layer, 0, 0)))

    crows = n_seq * tile
    scratch = [
        pltpu.VMEM((n_seq * rows, D_MODEL), jnp.bfloat16),
        pltpu.VMEM((n_seq * rows, IN_W), jnp.float32),
        pltpu.VMEM((N_Q_HEADS, crows, HEAD_DIM), jnp.bfloat16),
        pltpu.VMEM((N_KV_HEADS, n_seq * rows, HEAD_DIM), jnp.bfloat16),
        pltpu.VMEM((N_KV_HEADS, n_seq * rows, HEAD_DIM), jnp.bfloat16),
        pltpu.VMEM((tile + 2 * MARGIN, 256), jnp.float32),
        pltpu.VMEM((crows, D_MODEL), jnp.bfloat16),
        pltpu.VMEM((crows, D_MODEL), jnp.float32),
        pltpu.VMEM((2, crows, D_MODEL), jnp.float32),
        pltpu.VMEM((2, crows, D_MODEL), jnp.bfloat16),
    ]
    cfg = types.SimpleNamespace(layer=layer, tile=tile, n_seq=n_seq, halo=halo, seq_len=seq_len, n_steps=n_steps,
                                latent=latent, last=last)
    kern = functools.partial(_layer_kernel, cfg=cfg)
    outs = pl.pallas_call(
        kern,
        grid=(n_steps + 1,),
        in_specs=in_specs,
        out_specs=out_specs,
        out_shape=out_shape,
        input_output_aliases=aliases,
        scratch_shapes=scratch,
        compiler_params=pltpu.CompilerParams(
            dimension_semantics=("arbitrary",),
            vmem_limit_bytes=VMEM_LIMIT_BYTES),
        name=("latent" if latent else "context") + "_layer",
    )(*args)
    return outs


def _rope_tables(seq_len):
    rows = seq_len // GRID_W
    row = np.repeat(np.arange(rows), GRID_W).astype(np.float32)
    col = np.tile(np.arange(GRID_W), rows).astype(np.float32)
    inv = (np.float32(ROPE_BASE) ** (-np.arange(ROPE_PAIRS, dtype=np.float32) / np.float32(ROPE_PAIRS))).astype(
        np.float32)
    ang = np.concatenate([row[:, None] * inv, col[:, None] * inv], axis=-1)
    cos, sin = np.cos(ang.astype(np.float64)), np.sin(ang.astype(np.float64))
    cos_t = np.tile(np.concatenate([cos, cos], axis=-1), (1, 2)).astype(np.float32)
    sin_t = np.tile(np.concatenate([-sin, sin], axis=-1), (1, 2)).astype(np.float32)
    pad = ((HALO, HALO), (0, 0))
    return jnp.asarray(np.pad(cos_t, pad)), jnp.asarray(np.pad(sin_t, pad))


def _prep_params(norm1, norm2, w_in, w_out, attn_sink, pool_w, pool_scale, conv_dw, conv_b, conv_norm,
                 conv_pw, gm_norm, gm_ws, gm_b, w_mlp1, w_mlp2):
    depth = w_in.shape[0]
    bf = jnp.bfloat16
    gw = POOL_W // POOL_GROUPS
    pool_bd = jnp.zeros((depth, POOL_W, POOL_W), jnp.float32)
    for g in range(POOL_GROUPS):
        pool_bd = pool_bd.at[:, g * gw:(g + 1) * gw, g * gw:(g + 1) * gw].set(pool_w[:, g])
    gm_b_full = jnp.repeat(jnp.transpose(gm_b, (0, 2, 1)), GM_GROUP_W, axis=-1)
    return {
        'norm1': norm1.reshape(depth, 1, D_MODEL),
        'norm2': norm2.reshape(depth, 1, D_MODEL),
        'w_in': w_in.astype(bf),
        'w_out': w_out.astype(bf),
        'attn_sink': attn_sink,
        'pool_w': pool_bd.astype(bf),
        'pool_scale': pool_scale.reshape(depth, 1, POOL_W),
        'conv_dw': conv_dw,
        'conv_b': conv_b.reshape(depth, 1, CONV_W),
        'conv_norm': conv_norm.reshape(depth, 1, CONV_W),
        'conv_pw': conv_pw.astype(bf),
        'gm_norm': gm_norm.reshape(depth, 1, GM_W),
        'gm_ws': gm_ws.astype(bf),
        'gm_b': gm_b_full,
        'w_mlp1': w_mlp1.astype(bf),
        'w_mlp2': w_mlp2.astype(bf),
    }


def kernel(x_prompt, x_sample, cache_k, cache_v, c, c_ctx, w_ada, b_ada, norm1, norm2, w_in, w_out, attn_sink,
           pool_w, pool_scale, conv_dw, conv_b, conv_norm, conv_pw, gm_norm, gm_ws, gm_b, w_mlp1, w_mlp2,
           final_norm):
    depth = w_in.shape[0]
    dec_batch, dec_seq, _ = x_sample.shape
    batch, seq, _ = x_prompt.shape
    past = cache_k.shape[2]

    cond = jnp.zeros((MOD_ROWS, D_MODEL), jnp.float32)
    cond = cond.at[:dec_batch].set(c).at[CTX_MOD_ROW].set(c_ctx)
    mod_all = _ada_mod(cond, w_ada, b_ada).reshape(depth * MOD_ROWS, 1, N_MOD * D_MODEL)

    params = _prep_params(norm1, norm2, w_in, w_out, attn_sink, pool_w, pool_scale, conv_dw, conv_b,
                          conv_norm, conv_pw, gm_norm, gm_ws, gm_b, w_mlp1, w_mlp2)
    rope = _rope_tables(dec_seq)
    ck = cache_k.reshape(dec_batch, depth, past, KV_W)
    cv = cache_v.reshape(dec_batch, depth, past, KV_W)
    fnorm = final_norm.reshape(1, D_MODEL)

    xc, xs = x_prompt, x_sample
    new_k = jnp.zeros((batch, depth, seq, KV_W), jnp.float32)
    new_v = jnp.zeros((batch, depth, seq, KV_W), jnp.float32)
    for l in range(depth):
        last = l == depth - 1
        xc, new_k, new_v = _layer_call(xc, l, mod_all, params, tile=seq, latent=False, last=last,
                                       final_norm=fnorm if last else None, kv_acc=(new_k, new_v))
        (xs,) = _layer_call(xs, l, mod_all, params, tile=STEP_ROWS, latent=True, last=last,
                            rope=rope, cache=(ck, cv), final_norm=fnorm if last else None)
    new_k = new_k.reshape(batch, depth, seq, N_KV_HEADS, HEAD_DIM)
    new_v = new_v.reshape(batch, depth, seq, N_KV_HEADS, HEAD_DIM)
    return (xc, xs, new_k, new_v)
```

```python
import functools
import types

import jax
import jax.numpy as jnp
import numpy as np
from jax import lax
from jax.experimental import pallas as pl
from jax.experimental.pallas import tpu as pltpu

D_MODEL = 1024
DEPTH = 4
ATTN_W = 256
POOL_W = 256
CONV_W = 256
GM_W = 256
HEAD_DIM = 64
N_Q_HEADS = 4
N_KV_HEADS = 2
Q_PER_KV = 2
KV_W = 128
ATTN_BLOCK = 128
GRID_W = 64
ROPE_BASE = 10000.0
ROPE_PAIRS = 16
POOL_GROUPS = 4
POOL_SIZES = (2, 4, 8, 16)
CONV_WIDTH = 31
CONV_HALF = 15
GM_GROUPS = 4
GM_GROUP_W = 64
GM_CHUNK = 128
D_FF = 4096
N_MOD = 6
EPS = 1e-6
NEG_INF = -1e30
LOG2_E = 1.4426950408889634
IN_W = 1792
C_Q, C_K, C_V, C_POOL, C_CONV, C_GM = 0, 256, 384, 512, 768, 1280

HALO = 128
MARGIN = 16
POOL_EDGE = 16
MOD_ROWS = 16
CTX_MOD_ROW = 8
FF_CHUNK = 512
CONV_ROWS = 64
TAIL_ROWS = 128
STEP_ROWS = 512
VMEM_LIMIT_BYTES = 58 * 1024 * 1024


def _rms(x, g):
    ms = jnp.mean(x * x, axis=-1, keepdims=True)
    return x * lax.rsqrt(ms + EPS) * g


def _dot(a, b):
    return jnp.dot(a, b, preferred_element_type=jnp.float32)


def _dot_nt(a, b):
    return lax.dot_general(a, b, (((1,), (1,)), ((), ())), preferred_element_type=jnp.float32)


def _ada_kernel(cond_ref, w_ref, b_ref, o_ref):
    s = cond_ref[...]
    s = s * jax.nn.sigmoid(s)
    o_ref[0] = _dot(s.astype(jnp.bfloat16), w_ref[0].astype(jnp.bfloat16)) + b_ref[0]


def _ada_mod(cond, w_ada, b_ada):
    tn = 1536
    nt = (N_MOD * D_MODEL) // tn
    return pl.pallas_call(
        _ada_kernel,
        grid=(DEPTH, nt),
        in_specs=[
            pl.BlockSpec((MOD_ROWS, D_MODEL), lambda l, j: (0, 0)),
            pl.BlockSpec((1, D_MODEL, tn), lambda l, j: (l, 0, j)),
            pl.BlockSpec((1, 1, tn), lambda l, j: (l, 0, j)),
        ],
        out_specs=pl.BlockSpec((1, MOD_ROWS, tn), lambda l, j: (l, 0, j)),
        out_shape=jax.ShapeDtypeStruct((DEPTH, MOD_ROWS, N_MOD * D_MODEL), jnp.float32),
        compiler_params=pltpu.CompilerParams(
            dimension_semantics=("arbitrary", "arbitrary"),
            vmem_limit_bytes=VMEM_LIMIT_BYTES),
        name="ada_mod",
    )(cond, w_ada, b_ada.reshape(DEPTH, 1, N_MOD * D_MODEL))


def _mlp_half(r, cfg):
    tile, n_seq = cfg.tile, cfg.n_seq
    slot = 1 - pl.program_id(0) % 2
    g2 = r.modp[0][:, 5 * D_MODEL:6 * D_MODEL]
    n_chunks = D_FF // FF_CHUNK
    for q in range(n_seq):
        r.out[q] = r.x1_scr[q * tile:(q + 1) * tile, :]
    for c in range(n_chunks):
        a = _dot(r.h2_scr[slot], r.w1[0, :, c * FF_CHUNK:(c + 1) * FF_CHUNK])
        a = jnp.square(jnp.maximum(a, 0.0)).astype(jnp.bfloat16)
        yield
        w2_c = r.w2[0, c * FF_CHUNK:(c + 1) * FF_CHUNK, :]
        if c < n_chunks - 1:
            f = _dot(a, w2_c)
            if c == 0:
                r.acc_scr[...] = f
            else:
                r.acc_scr[...] += f
            yield
    x1 = jnp.concatenate([r.out[q] for q in range(n_seq)], axis=0)
    x2 = x1 + g2 * (r.acc_scr[...] + _dot(a, w2_c))
    yield
    if cfg.last:
        x2 = _rms(x2, r.fnorm[...])
    for q in range(n_seq):
        r.out[q] = x2[q * tile:(q + 1) * tile]


def _mixer_half(r, cfg):
    tile, n_seq, halo, seq_len, latent, layer = cfg.tile, cfg.n_seq, cfg.halo, cfg.seq_len, cfg.latent, cfg.layer
    rows = tile + 2 * halo
    step = pl.program_id(0)
    slot = step % 2
    cur = jnp.minimum(step, cfg.n_steps - 1)
    n_tiles = seq_len // tile
    ti = cur % n_tiles
    is_first = ti == 0
    is_last_tile = ti == n_tiles - 1
    h_scr, proj_scr, q_scr, k_scr, v_scr, slab_scr, mix_scr = (
        r.h_scr, r.proj_scr, r.q_scr, r.k_scr, r.v_scr, r.slab_scr, r.mix_scr)

    mod = r.mod[0]
    sh1 = mod[:, 0 * D_MODEL:1 * D_MODEL]
    sc1 = mod[:, 1 * D_MODEL:2 * D_MODEL]
    g1 = mod[:, 2 * D_MODEL:3 * D_MODEL]
    sh2 = mod[:, 3 * D_MODEL:4 * D_MODEL]
    sc2 = mod[:, 4 * D_MODEL:5 * D_MODEL]
    norm1 = r.norm1[0]
    norm2 = r.norm2[0]

    def norm_mod1(x):
        return (_rms(x, norm1) * (1.0 + sc1) + sh1).astype(jnp.bfloat16)

    c0 = halo
    c1 = halo + tile
    if halo:
        h_scr[0:halo, :] = norm_mod1(r.xprev[0])
        h_scr[c1:rows, :] = norm_mod1(r.xnext[0])
        h_scr[c0:c1, :] = norm_mod1(r.x[0])
        m0, m1 = c0 - MARGIN, c1 + MARGIN
        proj_scr[c0:c1, C_Q:C_K] = _dot(h_scr[c0:c1, :], r.w_in[0, :, C_Q:C_K])
        proj_scr[:, C_K:C_POOL] = _dot(h_scr[...], r.w_in[0, :, C_K:C_POOL])
        proj_scr[m0:m1, C_POOL:C_GM] = _dot(h_scr[m0:m1, :], r.w_in[0, :, C_POOL:C_GM])
        proj_scr[c0:c1, C_GM:IN_W] = _dot(h_scr[c0:c1, :], r.w_in[0, :, C_GM:IN_W])
    else:
        for q in range(n_seq):
            h_scr[q * tile:(q + 1) * tile, :] = norm_mod1(r.x[q])
        proj_scr[...] = _dot(h_scr[...], r.w_in[0])
    yield

    scale = HEAD_DIM ** -0.5 * LOG2_E
    if latent:
        row0 = ti * tile
        cos_t = r.cos[pl.ds(pl.multiple_of(row0, tile), rows), :]
        sin_t = r.sin[pl.ds(pl.multiple_of(row0, tile), rows), :]
        lane = lax.broadcasted_iota(jnp.int32, (1, 128), 1)
        first_half = (lane % HEAD_DIM) < (HEAD_DIM // 2)

        def rope(z, cs, sn):
            swapped = jnp.where(first_half, pltpu.roll(z, 96, 1), pltpu.roll(z, 32, 1))
            return z * cs + swapped * sn

    k_all = proj_scr[:, C_K:C_K + KV_W]
    if not latent:
        for q in range(n_seq):
            r.kout[q, 0] = k_all[q * tile:(q + 1) * tile]
            r.vout[q, 0] = proj_scr[q * tile:(q + 1) * tile, C_V:C_V + KV_W]
    if latent:
        k_all = rope(k_all, cos_t, sin_t)
    k_bf = k_all.astype(jnp.bfloat16)
    v_bf = proj_scr[:, C_V:C_V + KV_W].astype(jnp.bfloat16)
    for g in range(N_KV_HEADS):
        k_scr[g] = k_bf[:, g * HEAD_DIM:(g + 1) * HEAD_DIM]
        v_scr[g] = v_bf[:, g * HEAD_DIM:(g + 1) * HEAD_DIM]
    for q in range(n_seq):
        for half in range(2):
            qh = proj_scr[q * rows + c0:q * rows + c1, C_Q + half * 128:C_Q + (half + 1) * 128]
            if latent:
                qh = rope(qh, cos_t[c0:c1], sin_t[c0:c1])
            qh = (qh * scale).astype(jnp.bfloat16)
            for s in range(2):
                q_scr[2 * half + s, q * tile:(q + 1) * tile, :] = qh[:, s * HEAD_DIM:(s + 1) * HEAD_DIM]
    yield

    if latent:
        ck_bf = r.ck[0, 0].astype(jnp.bfloat16)
        cv_bf = r.cv[0, 0].astype(jnp.bfloat16)
        r_i = lax.broadcasted_iota(jnp.int32, (2 * ATTN_BLOCK, 3 * ATTN_BLOCK), 0) % ATTN_BLOCK
        c_i = lax.broadcasted_iota(jnp.int32, (2 * ATTN_BLOCK, 3 * ATTN_BLOCK), 1)
        band_ok = jnp.abs(r_i - (c_i - ATTN_BLOCK)) <= HALO
        n_qblk = tile // ATTN_BLOCK
        head_row = lax.broadcasted_iota(jnp.int32, (2 * ATTN_BLOCK, 1), 0) < ATTN_BLOCK
        for j in range(n_qblk):
            bias = jnp.where(band_ok, 0.0, NEG_INF)
            if j == 0:
                lo = jnp.where(is_first, ATTN_BLOCK, 0)
                bias = jnp.where(c_i >= lo, bias, NEG_INF)
            if j == n_qblk - 1:
                hi = jnp.where(is_last_tile, 2 * ATTN_BLOCK, 3 * ATTN_BLOCK)
                bias = jnp.where(c_i < hi, bias, NEG_INF)
            js = slice(j * ATTN_BLOCK, (j + 1) * ATTN_BLOCK)
            ks = slice(j * ATTN_BLOCK, (j + 3) * ATTN_BLOCK)
            for g in range(N_KV_HEADS):
                q2 = jnp.concatenate([q_scr[2 * g, js, :], q_scr[2 * g + 1, js, :]], axis=0)
                s_loc = _dot_nt(q2, k_scr[g, ks, :]) + bias
                s_ctx = _dot_nt(q2, ck_bf[:, g * HEAD_DIM:(g + 1) * HEAD_DIM])
                sink = jnp.where(head_row, r.sink[layer, 2 * g] * LOG2_E, r.sink[layer, 2 * g + 1] * LOG2_E)
                m = jnp.maximum(jnp.maximum(jnp.max(s_loc, axis=-1, keepdims=True),
                                            jnp.max(s_ctx, axis=-1, keepdims=True)), sink)
                yield
                p_loc = jnp.exp2(s_loc - m)
                p_ctx = jnp.exp2(s_ctx - m)
                den = (jnp.sum(p_loc, axis=-1, keepdims=True) + jnp.sum(p_ctx, axis=-1, keepdims=True)
                       + jnp.exp2(sink - m))
                yield
                o = (_dot(p_loc.astype(jnp.bfloat16), v_scr[g, ks, :])
                     + _dot(p_ctx.astype(jnp.bfloat16), cv_bf[:, g * HEAD_DIM:(g + 1) * HEAD_DIM]))
                o = (o / den).astype(jnp.bfloat16)
                mix_scr[js, (2 * g) * HEAD_DIM:(2 * g + 1) * HEAD_DIM] = o[:ATTN_BLOCK]
                mix_scr[js, (2 * g + 1) * HEAD_DIM:(2 * g + 2) * HEAD_DIM] = o[ATTN_BLOCK:]
                yield
    else:
        head_row = lax.broadcasted_iota(jnp.int32, (2 * tile, 1), 0) < tile
        for q in range(n_seq):
            qs = slice(q * tile, (q + 1) * tile)
            for g in range(N_KV_HEADS):
                q2 = jnp.concatenate([q_scr[2 * g, qs, :], q_scr[2 * g + 1, qs, :]], axis=0)
                s = _dot_nt(q2, k_scr[g, qs, :])
                sink = jnp.where(head_row, r.sink[layer, 2 * g] * LOG2_E, r.sink[layer, 2 * g + 1] * LOG2_E)
                m = jnp.maximum(jnp.max(s, axis=-1, keepdims=True), sink)
                yield
                p = jnp.exp2(s - m)
                den = jnp.sum(p, axis=-1, keepdims=True) + jnp.exp2(sink - m)
                yield
                o = (_dot(p.astype(jnp.bfloat16), v_scr[g, qs, :]) / den).astype(jnp.bfloat16)
                mix_scr[qs, (2 * g) * HEAD_DIM:(2 * g + 1) * HEAD_DIM] = o[:tile]
                mix_scr[qs, (2 * g + 1) * HEAD_DIM:(2 * g + 2) * HEAD_DIM] = o[tile:]
                yield

    srows = tile + 2 * MARGIN
    lo_grp = lax.broadcasted_iota(jnp.int32, (1, 128), 1) < POOL_W // POOL_GROUPS
    pool_cols = slice(ATTN_W, ATTN_W + POOL_W)
    conv_cols = slice(ATTN_W + POOL_W, ATTN_W + POOL_W + CONV_W)
    first_tap = MARGIN - CONV_HALF

    for q in range(n_seq):
        base = q * rows
        qs = slice(q * tile, (q + 1) * tile)

        def fill_slab(fn):
            slab_scr[MARGIN:MARGIN + tile, :] = fn(base + c0, base + c1)
            zeros = jnp.zeros((MARGIN, 256), jnp.float32)
            if halo:
                left = fn(c0 - MARGIN, c0)
                right = fn(c1, c1 + MARGIN)
                slab_scr[0:MARGIN, :] = jnp.where(is_first, zeros, left)
                slab_scr[MARGIN + tile:srows, :] = jnp.where(is_last_tile, zeros, right)
            else:
                slab_scr[0:MARGIN, :] = zeros
                slab_scr[MARGIN + tile:srows, :] = zeros

        fill_slab(lambda a, b: proj_scr[a:b, C_POOL:C_POOL + POOL_W])
        for hf in range(2):
            xs = slab_scr[:, hf * 128:(hf + 1) * 128]
            size_lo, size_hi = POOL_SIZES[2 * hf], POOL_SIZES[2 * hf + 1]
            wins = {}
            acc = xs
            shift_by = 1
            for size in POOL_SIZES:
                if size > size_hi:
                    break
                acc = acc + pltpu.roll(acc, shift_by, 0)
                shift_by = size
                if size in (size_lo, size_hi):
                    shift = size // 2 - 1
                    win = acc if shift == 0 else pltpu.roll(acc, srows - shift, 0)
                    wins[size] = win[MARGIN:MARGIN + tile]
            wsum = jnp.where(lo_grp, wins[size_lo], wins[size_hi])
            half_w = jnp.where(lo_grp, size_lo // 2, size_hi // 2)
            inv_full = jnp.where(lo_grp, 1.0 / size_lo, 1.0 / size_hi)

            def edge(row0):
                tpos = ti * tile + row0 + lax.broadcasted_iota(jnp.int32, (POOL_EDGE, 1), 0)
                cnt = jnp.minimum(tpos + half_w, seq_len) - jnp.maximum(tpos - half_w, 0)
                return wsum[row0:row0 + POOL_EDGE] / cnt.astype(jnp.float32)

            mean = jnp.concatenate(
                [edge(0), wsum[POOL_EDGE:tile - POOL_EDGE] * inv_full, edge(tile - POOL_EDGE)], axis=0)
            pooled = mean - xs[MARGIN:MARGIN + tile]
            mix_scr[qs, ATTN_W + hf * 128:ATTN_W + (hf + 1) * 128] = pooled.astype(jnp.bfloat16)
        yield

        def glu(a, b):
            av = proj_scr[a:b, C_CONV:C_CONV + CONV_W]
            gv = proj_scr[a:b, C_CONV + CONV_W:C_CONV + 2 * CONV_W]
            return av * jax.nn.sigmoid(gv)

        fill_slab(glu)
        for blk in range(tile // CONV_ROWS):
            b0 = blk * CONV_ROWS
            y = None
            for rr in range(8):
                z = None
                for a in range((first_tap + CONV_WIDTH + 7) // 8):
                    o = 8 * a + rr
                    if first_tap <= o < first_tap + CONV_WIDTH:
                        j = o - first_tap
                        term = slab_scr[b0 + 8 * a:b0 + 8 * a + CONV_ROWS + 8, :] * r.conv_dw[0, j:j + 1, :]
                        z = term if z is None else z + term
                zr = z[rr:rr + CONV_ROWS]
                y = zr if y is None else y + zr
            y = _rms(y + r.conv_b[0], r.conv_norm[0])
            y = y * jax.nn.sigmoid(y)
            mix_scr[q * tile + b0:q * tile + b0 + CONV_ROWS, conv_cols] = y.astype(jnp.bfloat16)
            yield

    pool_out = _dot(mix_scr[:, pool_cols], r.pool_w[0]) * r.pool_scale[0]
    mix_scr[:, pool_cols] = pool_out.astype(jnp.bfloat16)
    conv_out = _dot(mix_scr[:, conv_cols], r.conv_pw[0])
    mix_scr[:, conv_cols] = conv_out.astype(jnp.bfloat16)
    yield

    gm_b = r.gm_b[0]
    for q in range(n_seq):
        pr = slice(q * rows + c0, q * rows + c1)
        gu = jax.nn.gelu(proj_scr[pr, C_GM:C_GM + GM_W])
        gv = jax.nn.gelu(proj_scr[pr, C_GM + GM_W:C_GM + 2 * GM_W])
        gv = _rms(gv, r.gm_norm[0]).astype(jnp.bfloat16)
        for c in range(tile // GM_CHUNK):
            cs = slice(c * GM_CHUNK, (c + 1) * GM_CHUNK)
            parts = []
            for g in range(GM_GROUPS):
                parts.append(_dot(r.gm_ws[0, g], gv[cs, g * GM_GROUP_W:(g + 1) * GM_GROUP_W]))
            sv = jnp.concatenate(parts, axis=-1) + gm_b
            ms = slice(q * tile + c * GM_CHUNK, q * tile + (c + 1) * GM_CHUNK)
            mix_scr[ms, ATTN_W + POOL_W + CONV_W:D_MODEL] = (gu[cs] * sv).astype(jnp.bfloat16)
        yield

    mixed = _dot(mix_scr[...], r.w_out[0])
    for row in range(0, n_seq * tile, TAIL_ROWS):
        q, t0 = row // tile, row % tile
        x1 = r.x[q, t0:t0 + TAIL_ROWS, :] + g1 * mixed[row:row + TAIL_ROWS]
        r.x1_scr[row:row + TAIL_ROWS, :] = x1
        r.h2_scr[slot, row:row + TAIL_ROWS, :] = (_rms(x1, norm2) * (1.0 + sc2) + sh2).astype(jnp.bfloat16)
        yield


def _mixer_pieces(cfg):
    attn = (cfg.tile // ATTN_BLOCK if cfg.latent else cfg.n_seq) * N_KV_HEADS
    return 1 + 3 * attn +cfg.n_seq * (1 + cfg.tile // CONV_ROWS) + 1 + cfg.n_seq


def _layer_kernel(*refs, cfg):
    names = []
    if cfg.halo:
        names += ['xprev', 'x', 'xnext']
    else:
        names += ['x']
    names += ['mod']
    if cfg.latent:
        names += ['modp']
    names += ['norm1', 'norm2', 'w_in', 'w_out', 'sink', 'pool_w', 'pool_scale', 'conv_dw', 'conv_b', 'conv_norm',
              'conv_pw', 'gm_norm', 'gm_ws', 'gm_b', 'w1', 'w2']
    if cfg.latent:
        names += ['cos', 'sin', 'ck', 'cv']
    if cfg.last:
        names += ['fnorm']
    if not cfg.latent:
        names += ['kacc', 'vacc']
    names += ['out']
    if not cfg.latent:
        names += ['kout', 'vout']
    names += ['h_scr', 'proj_scr', 'q_scr', 'k_scr', 'v_scr', 'slab_scr', 'mix_scr', 'acc_scr', 'x1_scr', 'h2_scr']
    assert len(names) == len(refs)
    r = types.SimpleNamespace(**dict(zip(names, refs)))
    if not cfg.latent:
        r.modp = r.mod

    crows = cfg.n_seq * cfg.tile
    step = pl.program_id(0)

    @pl.when(step == 0)
    def _():
        r.x1_scr[...] = jnp.zeros((crows, D_MODEL), jnp.float32)
        r.h2_scr[1] = jnp.zeros((crows, D_MODEL), jnp.bfloat16)

    @pl.when(step == cfg.n_steps)
    def _():
        for _ in _mlp_half(r, cfg):
            pass

    @pl.when(step < cfg.n_steps)
    def _():
        mlp = _mlp_half(r, cfg)
        mixer = _mixer_half(r, cfg)
        n_mlp = 2 * (D_FF // FF_CHUNK)
        n_mix = _mixer_pieces(cfg)
        tail_mlp = 3
        next(mlp)
        next(mixer)
        done = 1
        for i in range(n_mix):
            next(mixer)
            target = ((i + 1) * (n_mlp - tail_mlp)) // n_mix
            while done < target:
                next(mlp)
                done += 1
        n_tail = crows // TAIL_ROWS
        for k, _ in enumerate(mixer):
            target = n_mlp - tail_mlp + ((k + 1) * tail_mlp) // n_tail
            while done < min(target, n_mlp):
                next(mlp)
                done += 1
        for _ in mlp:
            pass


def _const_spec(shape, index):
    return pl.BlockSpec(shape, lambda s: index, pipeline_mode=pl.Buffered(1))


def _layer_call(x, layer, mod_all, params, *, tile, latent, last, rope=None, cache=None, final_norm=None,
                kv_acc=None):
    batch, seq_len, _ = x.shape
    halo = HALO if latent else 0
    rows = tile + 2 * halo
    n_tiles = seq_len // tile
    n_seq = STEP_ROWS // tile
    assert seq_len % tile == 0 and tile % ATTN_BLOCK == 0 and STEP_ROWS % tile == 0
    if latent:
        assert n_seq == 1
    else:
        assert n_tiles == 1 and batch % n_seq == 0
    n_steps = (batch // n_seq) * n_tiles
    last_step = n_steps - 1

    def cur(s):
        return jnp.minimum(s, last_step)

    def prev(s):
        return jnp.maximum(s - 1, 0)

    in_specs = []
    args = []
    if halo:
        hb = tile // halo
        n_hblk = seq_len // halo
        in_specs.append(pl.BlockSpec(
            (1, halo, D_MODEL),
            lambda s: (cur(s) // n_tiles, jnp.maximum((cur(s) % n_tiles) * hb - 1, 0), 0)))
        args.append(x)
    in_specs.append(pl.BlockSpec((n_seq, tile, D_MODEL), lambda s: (cur(s) // n_tiles, cur(s) % n_tiles, 0)))
    args.append(x)
    if halo:
        in_specs.append(pl.BlockSpec(
            (1, halo, D_MODEL),
            lambda s: (cur(s) // n_tiles, jnp.minimum((cur(s) % n_tiles + 1) * hb, n_hblk - 1), 0)))
        args.append(x)

    if latent:
        in_specs.append(pl.BlockSpec((1, 1, N_MOD * D_MODEL),
                                     lambda s: (layer * MOD_ROWS + cur(s) // n_tiles, 0, 0)))
        args.append(mod_all)
        in_specs.append(pl.BlockSpec((1, 1, N_MOD * D_MODEL),
                                     lambda s: (layer * MOD_ROWS + prev(s) // n_tiles, 0, 0)))
        args.append(mod_all)
    else:
        in_specs.append(pl.BlockSpec((1, 1, N_MOD * D_MODEL), lambda s: (layer * MOD_ROWS + CTX_MOD_ROW, 0, 0)))
        args.append(mod_all)

    def add_const(arr, block):
        idx = (layer,) + (0,) * (len(block) - 1)
        in_specs.append(_const_spec(block, idx))
        args.append(arr)

    add_const(params['norm1'], (1, 1, D_MODEL))
    add_const(params['norm2'], (1, 1, D_MODEL))
    add_const(params['w_in'], (1, D_MODEL, IN_W))
    add_const(params['w_out'], (1, D_MODEL, D_MODEL))
    in_specs.append(pl.BlockSpec(memory_space=pltpu.SMEM))
    args.append(params['attn_sink'])
    add_const(params['pool_w'], (1, POOL_W, POOL_W))
    add_const(params['pool_scale'], (1, 1, POOL_W))
    add_const(params['conv_dw'], (1, CONV_WIDTH, CONV_W))
    add_const(params['conv_b'], (1, 1, CONV_W))
    add_const(params['conv_norm'], (1, 1, CONV_W))
    add_const(params['conv_pw'], (1, CONV_W, CONV_W))
    add_const(params['gm_norm'], (1, 1, GM_W))
    add_const(params['gm_ws'], (1, GM_GROUPS, GM_CHUNK, GM_CHUNK))
    add_const(params['gm_b'], (1, GM_CHUNK, GM_W))
    add_const(params['w_mlp1'], (1, D_MODEL, D_FF))
    add_const(params['w_mlp2'], (1, D_FF, D_MODEL))
    if latent:
        cos_t, sin_t = rope
        in_specs.append(_const_spec(cos_t.shape, (0, 0)))
        args.append(cos_t)
        in_specs.append(_const_spec(sin_t.shape, (0, 0)))
        args.append(sin_t)
        ck, cv = cache
        past = ck.shape[2]
        in_specs.append(pl.BlockSpec((1, 1, past, KV_W), lambda s: (cur(s) // n_tiles, layer, 0, 0)))
        args.append(ck)
        in_specs.append(pl.BlockSpec((1, 1, past, KV_W), lambda s: (cur(s) // n_tiles, layer, 0, 0)))
        args.append(cv)
    if last:
        in_specs.append(_const_spec((1, D_MODEL), (0, 0)))
        args.append(final_norm)

    out_shape = [jax.ShapeDtypeStruct(x.shape, jnp.float32)]
    out_specs = [pl.BlockSpec((n_seq, tile, D_MODEL), lambda s: (prev(s) // n_tiles, prev(s) % n_tiles, 0))]
    aliases = {}
    if not latent:
        for i, acc in enumerate(kv_acc):
            in_specs.append(pl.BlockSpec(memory_space=pl.ANY))
            args.append(acc)
            aliases[len(args) - 1] = 1 + i
            out_shape.append(jax.ShapeDtypeStruct(acc.shape, jnp.float32))
            out_specs.append(pl.BlockSpec((n_seq, 1, tile, KV_W), lambda s: (cur(s), layer, 0, 0)))

    crows = n_seq * tile
    scratch = [
        pltpu.VMEM((n_seq * rows, D_MODEL), jnp.bfloat16),
        pltpu.VMEM((n_seq * rows, IN_W), jnp.float32),
        pltpu.VMEM((N_Q_HEADS, crows, HEAD_DIM), jnp.bfloat16),
        pltpu.VMEM((N_KV_HEADS, n_seq * rows, HEAD_DIM), jnp.bfloat16),
        pltpu.VMEM((N_KV_HEADS, n_seq * rows, HEAD_DIM), jnp.bfloat16),
        pltpu.VMEM((tile + 2 * MARGIN, 256), jnp.float32),
        pltpu.VMEM((crows, D_MODEL), jnp.bfloat16),
        pltpu.VMEM((crows, D_MODEL), jnp.float32),
        pltpu.VMEM((crows, D_MODEL), jnp.float32),
        pltpu.VMEM((2, crows, D_MODEL), jnp.bfloat16),
    ]
    cfg = types.SimpleNamespace(layer=layer, tile=tile, n_seq=n_seq, halo=halo, seq_len=seq_len, n_steps=n_steps,
                                latent=latent, last=last)
    kern = functools.partial(_layer_kernel, cfg=cfg)
    outs = pl.pallas_call(
        kern,
        grid=(n_steps + 1,),
        in_specs=in_specs,
        out_specs=out_specs,
        out_shape=out_shape,
        input_output_aliases=aliases,
        scratch_shapes=scratch,
        compiler_params=pltpu.CompilerParams(
            dimension_semantics=("arbitrary",),
            vmem_limit_bytes=VMEM_LIMIT_BYTES),
        name=("latent" if latent else "context") + "_layer",
    )(*args)
    return outs


def _rope_tables(seq_len):
    rows = seq_len // GRID_W
    row = np.repeat(np.arange(rows), GRID_W).astype(np.float32)
    col = np.tile(np.arange(GRID_W), rows).astype(np.float32)
    inv = (np.float32(ROPE_BASE) ** (-np.arange(ROPE_PAIRS, dtype=np.float32) / np.float32(ROPE_PAIRS))).astype(
        np.float32)
    ang = np.concatenate([row[:, None] * inv, col[:, None] * inv], axis=-1)
    cos, sin = np.cos(ang.astype(np.float64)), np.sin(ang.astype(np.float64))
    cos_t = np.tile(np.concatenate([cos, cos], axis=-1), (1, 2)).astype(np.float32)
    sin_t = np.tile(np.concatenate([-sin, sin], axis=-1), (1, 2)).astype(np.float32)
    pad = ((HALO, HALO), (0, 0))
    return jnp.asarray(np.pad(cos_t, pad)), jnp.asarray(np.pad(sin_t, pad))


def _prep_params(norm1, norm2, w_in, w_out, attn_sink, pool_w, pool_scale, conv_dw, conv_b, conv_norm,
                 conv_pw, gm_norm, gm_ws, gm_b, w_mlp1, w_mlp2):
    depth = w_in.shape[0]
    bf = jnp.bfloat16
    gw = POOL_W // POOL_GROUPS
    pool_bd = jnp.zeros((depth, POOL_W, POOL_W), jnp.float32)
    for g in range(POOL_GROUPS):
        pool_bd = pool_bd.at[:, g * gw:(g + 1) * gw, g * gw:(g + 1) * gw].set(pool_w[:, g])
    gm_b_full = jnp.repeat(jnp.transpose(gm_b, (0, 2, 1)), GM_GROUP_W, axis=-1)
    return {
        'norm1': norm1.reshape(depth, 1, D_MODEL),
        'norm2': norm2.reshape(depth, 1, D_MODEL),
        'w_in': w_in.astype(bf),
        'w_out': w_out.astype(bf),
        'attn_sink': attn_sink,
        'pool_w': pool_bd.astype(bf),
        'pool_scale': pool_scale.reshape(depth, 1, POOL_W),
        'conv_dw': conv_dw,
        'conv_b': conv_b.reshape(depth, 1, CONV_W),
        'conv_norm': conv_norm.reshape(depth, 1, CONV_W),
        'conv_pw': conv_pw.astype(bf),
        'gm_norm': gm_norm.reshape(depth, 1, GM_W),
        'gm_ws': gm_ws.astype(bf),
        'gm_b': gm_b_full,
        'w_mlp1': w_mlp1.astype(bf),
        'w_mlp2': w_mlp2.astype(bf),
    }


def kernel(x_prompt, x_sample, cache_k, cache_v, c, c_ctx, w_ada, b_ada, norm1, norm2, w_in, w_out, attn_sink,
           pool_w, pool_scale, conv_dw, conv_b, conv_norm, conv_pw, gm_norm, gm_ws, gm_b, w_mlp1, w_mlp2,
           final_norm):
    depth = w_in.shape[0]
    dec_batch, dec_seq, _ = x_sample.shape
    batch, seq, _ = x_prompt.shape
    past = cache_k.shape[2]

    cond = jnp.zeros((MOD_ROWS, D_MODEL), jnp.float32)
    cond = cond.at[:dec_batch].set(c).at[CTX_MOD_ROW].set(c_ctx)
    mod_all = _ada_mod(cond, w_ada, b_ada).reshape(depth * MOD_ROWS, 1, N_MOD * D_MODEL)

    params = _prep_params(norm1, norm2, w_in, w_out, attn_sink, pool_w, pool_scale, conv_dw, conv_b,
                          conv_norm, conv_pw, gm_norm, gm_ws, gm_b, w_mlp1, w_mlp2)
    rope = _rope_tables(dec_seq)
    ck = cache_k.reshape(dec_batch, depth, past, KV_W)
    cv = cache_v.reshape(dec_batch, depth, past, KV_W)
    fnorm = final_norm.reshape(1, D_MODEL)

    xc, xs = x_prompt, x_sample
    new_k = jnp.zeros((batch, depth, seq, KV_W), jnp.float32)
    new_v = jnp.zeros((batch, depth, seq, KV_W), jnp.float32)
    for l in range(depth):
        last = l == depth - 1
        xc, new_k, new_v = _layer_call(xc, l, mod_all, params, tile=seq, latent=False, last=last,
                                       final_norm=fnorm if last else None, kv_acc=(new_k, new_v))
        (xs,) = _layer_call(xs, l, mod_all, params, tile=STEP_ROWS, latent=True, last=last,
                            rope=rope, cache=(ck, cv), final_norm=fnorm if last else None)
    new_k = new_k.reshape(batch, depth, seq, N_KV_HEADS, HEAD_DIM)
    new_v = new_v.reshape(batch, depth, seq, N_KV_HEADS, HEAD_DIM)
    return (xc, xs, new_k, new_v)
```
